```python
import math
import jax
import jax.numpy as jnp
from jax import lax
import numpy as np


D_MODEL = 1024
BATCH = 8
SEQ = 2048
DEPTH = 2
DEC_BATCH = 128
DEC_SEQ = 8
PAST_LEN = 2048
PAGE_SIZE = 128

N_MIXERS = 2
N_RG_LAYERS = (DEPTH + 1) // 2
N_ATTN_LAYERS = DEPTH // 2
N_META = 16
D_RNN = D_MODEL
RG_BLOCKS = 8
RG_BLOCK = D_RNN // RG_BLOCKS
CONV_W = 4
RG_C = 8.0
N_HEADS = 8
HEAD_DIM = D_MODEL // N_HEADS
Q_BLOCK = 128
ATTN_SCALE = HEAD_DIM ** -0.5
PEER_HEADS = 8
PEER_TOPK = 16
N_KEYS = 128
N_EXPERTS = N_KEYS * N_KEYS
D_KEY = 256
D_HALF = D_KEY // 2
PEER_CHUNK = 128
ALPHA = (2.0 * DEPTH) ** 0.25
BETA = (8.0 * DEPTH) ** -0.25
LN_EPS = 1e-5

kernel_name = 'hybrid_rglru_fox_peer_step'


def layer_norm(x, g, b):
    xf = x.astype(jnp.float32)
    mu = jnp.mean(xf, axis=-1, keepdims=True)
    var = jnp.mean(jnp.square(xf - mu), axis=-1, keepdims=True)
    return ((xf - mu) * lax.rsqrt(var + LN_EPS) * g + b).astype(x.dtype)


def rglru_mixer(x, conv_buf, h0, w_in, conv_w, conv_b, w_a, b_a, w_x, b_x, lam, w_out):
    B, T, _ = x.shape
    xin = x @ w_in
    gate = jax.nn.gelu(xin[..., :D_RNN])
    rec = xin[..., D_RNN:]
    xp = jnp.concatenate([conv_buf.astype(rec.dtype), rec], axis=1)
    rec = conv_b + sum(xp[:, k:k + T] * conv_w[k] for k in range(CONV_W))
    new_buf = xp[:, T:]
    rb = rec.reshape(B, T, RG_BLOCKS, RG_BLOCK)
    r = jax.nn.sigmoid((jnp.einsum('btnc,ncd->btnd', rb, w_a).reshape(B, T, D_RNN) + b_a).astype(jnp.float32))
    i = jax.nn.sigmoid((jnp.einsum('btnc,ncd->btnd', rb, w_x).reshape(B, T, D_RNN) + b_x).astype(jnp.float32))
    log_a = -RG_C * r * jax.nn.softplus(-lam.astype(jnp.float32))
    a = jnp.exp(log_a)
    bx = jnp.sqrt(-jnp.expm1(2.0 * log_a)) * (i * rec.astype(jnp.float32))

    def step(h, ab):
        at, bt = ab
        h = at * h + bt
        return h, h

    hT, hs = lax.scan(step, h0.astype(jnp.float32), (a.swapaxes(0, 1), bx.swapaxes(0, 1)))
    y = hs.swapaxes(0, 1).astype(x.dtype) * gate
    return y @ w_out, new_buf, hT.astype(x.dtype)


def fox_project(x, w_in, b_f):
    B, T, _ = x.shape
    z = x @ w_in
    q = z[..., :D_MODEL].reshape(B, T, N_HEADS, HEAD_DIM)
    k = z[..., D_MODEL:2 * D_MODEL].reshape(B, T, N_HEADS, HEAD_DIM)
    v = z[..., 2 * D_MODEL:3 * D_MODEL].reshape(B, T, N_HEADS, HEAD_DIM)
    logf = jax.nn.log_sigmoid((z[..., 3 * D_MODEL:] + b_f).astype(jnp.float32))
    return q, k, v, logf


def fox_attend(qb, cq, qpos, kk, vv, ck, kpos):
    s = jnp.einsum('bqhd,bkhd->bhqk', qb, kk).astype(jnp.float32) * ATTN_SCALE
    s = s + jnp.transpose(cq, (0, 2, 1))[..., :, None] - jnp.transpose(ck, (0, 2, 1))[..., None, :]
    s = jnp.where(kpos[None, :] <= qpos[:, None], s, -jnp.inf)
    p = jax.nn.softmax(s, axis=-1).astype(vv.dtype)
    return jnp.einsum('bhqk,bkhd->bqhd', p, vv)


def fox_prompt_attn(q, k, v, logf):
    B, T = q.shape[:2]
    n_real = T - N_META
    nb = n_real // Q_BLOCK
    c = jnp.cumsum(logf, axis=1)
    kpos = jnp.arange(T)
    meta_pos = jnp.arange(N_META)
    out_m = fox_attend(q[:, :N_META], c[:, :N_META], meta_pos, k[:, :N_META], v[:, :N_META], c[:, :N_META], meta_pos)
    qr = q[:, N_META:].reshape(B, nb, Q_BLOCK, N_HEADS, HEAD_DIM).swapaxes(0, 1)
    cr = c[:, N_META:].reshape(B, nb, Q_BLOCK, N_HEADS).swapaxes(0, 1)
    qpos = (N_META + jnp.arange(n_real)).reshape(nb, Q_BLOCK)
    outs = lax.map(lambda a: fox_attend(a[0], a[1], a[2], k, v, c, kpos), (qr, cr, qpos))
    out_r = outs.swapaxes(0, 1).reshape(B, n_real, N_HEADS, HEAD_DIM)
    return jnp.concatenate([out_m, out_r], axis=1)


def fox_sample_attn(q, k, v, logf, ck_pool, cv_pool, cl_pool, page_table):
    DB, S = q.shape[:2]
    kp = ck_pool[page_table].reshape(DB, -1, N_HEADS, HEAD_DIM)
    vp = cv_pool[page_table].reshape(DB, -1, N_HEADS, HEAD_DIM)
    lp = cl_pool[page_table].reshape(DB, -1, N_HEADS)
    P = kp.shape[1]
    c = jnp.cumsum(jnp.concatenate([lp.astype(jnp.float32), logf], axis=1), axis=1)
    c_past = jnp.transpose(c[:, :P], (0, 2, 1))
    c_new = jnp.transpose(c[:, P:], (0, 2, 1))
    s_past = jnp.einsum('bqhd,bkhd->bhqk', q, kp).astype(jnp.float32) * ATTN_SCALE
    s_past = s_past + c_new[..., :, None] - c_past[..., None, :]
    s_new = jnp.einsum('bqhd,bkhd->bhqk', q, k).astype(jnp.float32) * ATTN_SCALE
    s_new = s_new + c_new[..., :, None] - c_new[..., None, :]
    causal = jnp.arange(S)[None, :] <= jnp.arange(S)[:, None]
    s_new = jnp.where(causal, s_new, -jnp.inf)
    p = jax.nn.softmax(jnp.concatenate([s_past, s_new], axis=-1), axis=-1).astype(v.dtype)
    return jnp.einsum('bhqk,bkhd->bqhd', p[..., :P], vp) + jnp.einsum('bhqk,bkhd->bqhd', p[..., P:], v)


def peer_ffn(x, w_q, keys1, keys2, u, v):
    shape = x.shape
    xf = x.reshape(-1, D_MODEL)
    n = xf.shape[0]
    pad = (-n) % PEER_CHUNK
    xc = jnp.pad(xf, ((0, pad), (0, 0))).reshape(-1, PEER_CHUNK, D_MODEL)

    def chunk(xb):
        q = (xb @ w_q).reshape(PEER_CHUNK, PEER_HEADS, D_KEY)
        s1 = jnp.einsum('thc,hkc->thk', q[..., :D_HALF], keys1).astype(jnp.float32)
        s2 = jnp.einsum('thc,hkc->thk', q[..., D_HALF:], keys2).astype(jnp.float32)
        v1, i1 = lax.top_k(s1, PEER_TOPK)
        v2, i2 = lax.top_k(s2, PEER_TOPK)
        cand = (v1[..., :, None] + v2[..., None, :]).reshape(PEER_CHUNK, PEER_HEADS, PEER_TOPK * PEER_TOPK)
        cand_idx = (i1[..., :, None] * N_KEYS + i2[..., None, :]).reshape(PEER_CHUNK, PEER_HEADS, PEER_TOPK * PEER_TOPK)
        sc, j = lax.top_k(cand, PEER_TOPK)
        idx = jnp.take_along_axis(cand_idx, j, axis=-1)
        g = jax.nn.softmax(sc, axis=-1)
        act = jax.nn.gelu(jnp.einsum('thkd,td->thk', u[idx], xb).astype(jnp.float32))
        w = (g * act).astype(xb.dtype)
        return jnp.einsum('thk,thkd->td', w, v[idx])

    out = lax.map(chunk, xc).reshape(-1, D_MODEL)[:n]
    return out.reshape(shape)


def setup_inputs(seed: int = 0) -> dict:
    key = jax.random.key(seed)
    ks = jax.random.split(key, 32)
    f32 = jnp.float32

    def nrm(k, shape, s):
        return jax.random.normal(k, shape, f32) * s

    n_pages = PAST_LEN // PAGE_SIZE
    n_used = DEC_BATCH * n_pages
    n_pool = n_used + n_used // 4
    x_prompt = nrm(ks[0], (BATCH, SEQ, D_MODEL), 1.0)
    x_sample = nrm(ks[1], (DEC_BATCH, DEC_SEQ, D_MODEL), 1.0)
    cache_k = nrm(ks[2], (N_ATTN_LAYERS, n_pool, PAGE_SIZE, N_HEADS, HEAD_DIM), 1.0)
    cache_v = nrm(ks[3], (N_ATTN_LAYERS, n_pool, PAGE_SIZE, N_HEADS, HEAD_DIM), 1.0)
    cache_logf = jax.nn.log_sigmoid(4.0 + jax.random.normal(ks[4], (N_ATTN_LAYERS, n_pool, PAGE_SIZE, N_HEADS), f32))
    state_h = nrm(ks[5], (N_RG_LAYERS, DEC_BATCH, D_RNN), 0.5)
    state_conv = nrm(ks[6], (N_RG_LAYERS, DEC_BATCH, CONV_W - 1, D_RNN), 1.0)
    page_table = jax.random.permutation(ks[7], n_pool)[:n_used].reshape(DEC_BATCH, n_pages).astype(jnp.int32)
    meta_tokens = nrm(ks[8], (N_META, D_MODEL), 1.0)
    rg_w_in = nrm(ks[9], (N_RG_LAYERS, D_MODEL, 2 * D_RNN), D_MODEL ** -0.5)
    rg_conv_w = nrm(ks[10], (N_RG_LAYERS, CONV_W, D_RNN), CONV_W ** -0.5)
    rg_conv_b = nrm(ks[11], (N_RG_LAYERS, D_RNN), 0.01)
    rg_w_a = nrm(ks[12], (N_RG_LAYERS, RG_BLOCKS, RG_BLOCK, RG_BLOCK), RG_BLOCK ** -0.5)
    rg_b_a = nrm(ks[13], (N_RG_LAYERS, D_RNN), 0.01)
    rg_w_x = nrm(ks[14], (N_RG_LAYERS, RG_BLOCKS, RG_BLOCK, RG_BLOCK), RG_BLOCK ** -0.5)
    rg_b_x = nrm(ks[15], (N_RG_LAYERS, D_RNN), 0.01)
    a_c = jax.random.uniform(ks[16], (N_RG_LAYERS, D_RNN), f32, 0.9, 0.999)
    a0 = a_c ** (1.0 / RG_C)
    rg_lambda = jnp.log(a0) - jnp.log1p(-a0)
    rg_w_out = nrm(ks[17], (N_RG_LAYERS, D_RNN, D_MODEL), D_RNN ** -0.5 * BETA)
    col_scale = jnp.concatenate([jnp.ones((2 * D_MODEL,), f32), jnp.full((D_MODEL,), BETA, f32), jnp.ones((N_HEADS,), f32)])
    fox_w_in = nrm(ks[18], (N_ATTN_LAYERS, D_MODEL, 3 * D_MODEL + N_HEADS), D_MODEL ** -0.5) * col_scale
    fox_b_f = 2.0 + 4.0 * jax.random.uniform(ks[19], (N_ATTN_LAYERS, N_HEADS), f32)
    fox_w_o = nrm(ks[20], (N_ATTN_LAYERS, D_MODEL, D_MODEL), D_MODEL ** -0.5 * BETA)
    peer_w_q = nrm(ks[21], (DEPTH, D_MODEL, PEER_HEADS * D_KEY), D_MODEL ** -0.5)
    peer_keys1 = nrm(ks[22], (DEPTH, PEER_HEADS, N_KEYS, D_HALF), D_HALF ** -0.5)
    peer_keys2 = nrm(ks[23], (DEPTH, PEER_HEADS, N_KEYS, D_HALF), D_HALF ** -0.5)
    peer_u = nrm(ks[24], (DEPTH, N_EXPERTS, D_MODEL), D_MODEL ** -0.5)
    peer_v = nrm(ks[25], (DEPTH, N_EXPERTS, D_MODEL), BETA)
    ln_mix_g = 1.0 + nrm(ks[26], (DEPTH, D_MODEL), 0.05)
    ln_mix_b = nrm(ks[27], (DEPTH, D_MODEL), 0.02)
    ln_ffn_g = 1.0 + nrm(ks[28], (DEPTH, D_MODEL), 0.05)
    ln_ffn_b = nrm(ks[29], (DEPTH, D_MODEL), 0.02)
    return {'x_prompt': x_prompt, 'x_sample': x_sample, 'cache_k': cache_k, 'cache_v': cache_v,
            'cache_logf': cache_logf, 'state_h': state_h, 'state_conv': state_conv, 'page_table': page_table,
            'meta_tokens': meta_tokens, 'rg_w_in': rg_w_in, 'rg_conv_w': rg_conv_w, 'rg_conv_b': rg_conv_b,
            'rg_w_a': rg_w_a, 'rg_b_a': rg_b_a, 'rg_w_x': rg_w_x, 'rg_b_x': rg_b_x, 'rg_lambda': rg_lambda,
            'rg_w_out': rg_w_out, 'fox_w_in': fox_w_in, 'fox_b_f': fox_b_f, 'fox_w_o': fox_w_o,
            'peer_w_q': peer_w_q, 'peer_keys1': peer_keys1, 'peer_keys2': peer_keys2, 'peer_u': peer_u,
            'peer_v': peer_v, 'ln_mix_g': ln_mix_g, 'ln_mix_b': ln_mix_b, 'ln_ffn_g': ln_ffn_g, 'ln_ffn_b': ln_ffn_b}


def reference(x_prompt, x_sample, cache_k, cache_v, cache_logf, state_h, state_conv, page_table,
              meta_tokens, rg_w_in, rg_conv_w, rg_conv_b, rg_w_a, rg_b_a, rg_w_x, rg_b_x, rg_lambda,
              rg_w_out, fox_w_in, fox_b_f, fox_w_o, peer_w_q, peer_keys1, peer_keys2, peer_u, peer_v,
              ln_mix_g, ln_mix_b, ln_ffn_g, ln_ffn_b):
    B = x_prompt.shape[0]
    DB, S = x_sample.shape[:2]
    meta = jnp.broadcast_to(meta_tokens.astype(x_prompt.dtype), (B, N_META, D_MODEL))
    hp = jnp.concatenate([meta, x_prompt], axis=1)
    T = hp.shape[1]
    hs = x_sample
    sh_p, sc_p, k_p, v_p, lf_p = [], [], [], [], []
    sh_s, sc_s, k_s, v_s, lf_s = [], [], [], [], []
    for i in range(DEPTH):
        l = i // N_MIXERS
        if i % N_MIXERS == 0:
            rg = (rg_w_in[l], rg_conv_w[l], rg_conv_b[l], rg_w_a[l], rg_b_a[l], rg_w_x[l], rg_b_x[l], rg_lambda[l], rg_w_out[l])
            zero_buf = jnp.zeros((B, CONV_W - 1, D_RNN), hp.dtype)
            zero_h = jnp.zeros((B, D_RNN), hp.dtype)
            mp, buf_p, h_p = rglru_mixer(hp, zero_buf, zero_h, *rg)
            ms, buf_s, h_s = rglru_mixer(hs, state_conv[l], state_h[l], *rg)
            sh_p.append(h_p)
            sc_p.append(buf_p)
            sh_s.append(h_s)
            sc_s.append(buf_s)
        else:
            qp, kp, vp, lp = fox_project(hp, fox_w_in[l], fox_b_f[l])
            mp = fox_prompt_attn(qp, kp, vp, lp).reshape(B, T, D_MODEL) @ fox_w_o[l]
            qs, ks_, vs, ls = fox_project(hs, fox_w_in[l], fox_b_f[l])
            ms = fox_sample_attn(qs, ks_, vs, ls, cache_k[l], cache_v[l], cache_logf[l], page_table).reshape(DB, S, D_MODEL) @ fox_w_o[l]
            k_p.append(kp)
            v_p.append(vp)
            lf_p.append(lp.astype(hp.dtype))
            k_s.append(ks_)
            v_s.append(vs)
            lf_s.append(ls.astype(hs.dtype))
        hp = layer_norm(ALPHA * hp + mp, ln_mix_g[i], ln_mix_b[i])
        hs = layer_norm(ALPHA * hs + ms, ln_mix_g[i], ln_mix_b[i])
        hp = layer_norm(ALPHA * hp + peer_ffn(hp, peer_w_q[i], peer_keys1[i], peer_keys2[i], peer_u[i], peer_v[i]), ln_ffn_g[i], ln_ffn_b[i])
        hs = layer_norm(ALPHA * hs + peer_ffn(hs, peer_w_q[i], peer_keys1[i], peer_keys2[i], peer_u[i], peer_v[i]), ln_ffn_g[i], ln_ffn_b[i])
    y_prompt = hp[:, N_META:]
    y_sample = hs
    return (y_prompt, y_sample,
            jnp.stack(sh_p), jnp.stack(sc_p), jnp.stack(k_p), jnp.stack(v_p), jnp.stack(lf_p),
            jnp.stack(sh_s), jnp.stack(sc_s), jnp.stack(k_s), jnp.stack(v_s), jnp.stack(lf_s))
```

```python
import functools

import jax
import jax.numpy as jnp
from jax import lax
from jax.experimental import pallas as pl
from jax.experimental.pallas import tpu as pltpu

F32 = jnp.float32
BF16 = jnp.bfloat16

D = 1024
BATCH = 8
N_META = 16
T_PROMPT = 2064
DEC_BATCH = 128
DEC_SEQ = 8
N_HEADS = 8
HEAD_DIM = 128
PAGE_SIZE = 128
N_PAGES = 16
RG_BLOCKS = 8
RG_BLOCK = 128
RG_C = 8.0
CONV_W = 4
PEER_HEADS = 8
N_KEYS = 128
N_EXPERTS = N_KEYS * N_KEYS
PEER_TOPK = 16
DEPTH = 2
ALPHA = (2.0 * DEPTH) ** 0.25
LN_EPS = 1e-5
ATTN_SCALE = HEAD_DIM ** -0.5

SUBLANES = 8
LANES = 128
VMEM_LIMIT = 56 * 1024 * 1024

N_PROMPT = BATCH * T_PROMPT
TOK_TILE = 512
SAMPLE_OFF = 16896
N_SAMPLE = DEC_BATCH * DEC_SEQ
N_TOT = SAMPLE_OFF + N_SAMPLE
SAMPLE_BLK0 = SAMPLE_OFF // TOK_TILE

RG_TILE = 344
FOX_TILE = 688
ATT_TQ = 344
ATT_TK = 256
T_KPAD = 2304
EXP_TILE = 1024
N_CAND = 17
FOX_NPAD = 3 * D + LANES

NT_DIMS = (((1,), (1,)), ((), ()))


def _gelu(x):
    return 0.5 * x * (1.0 + jnp.tanh(0.7978845608028654 * (x + 0.044715 * (x * x * x))))


def _sigmoid(x):
    return 1.0 / (1.0 + jnp.exp(-x))


def _softplus(x):
    return jnp.maximum(x, 0.0) + jnp.log1p(jnp.exp(-jnp.abs(x)))


def _layer_norm(x, g, b):
    mu = jnp.mean(x, axis=-1, keepdims=True)
    xc = x - mu
    var = jnp.mean(xc * xc, axis=-1, keepdims=True)
    return xc * lax.rsqrt(var + LN_EPS) * g + b


def _full(shape):
    n = len(shape)
    return pl.BlockSpec(shape, lambda *_: (0,) * n)


def _params(sem):
    return pltpu.CompilerParams(dimension_semantics=sem, vmem_limit_bytes=VMEM_LIMIT)


def _rg_gates(rec, wax_ref, ba_ref, bx_ref, lam_ref):
    a_parts, i_parts = [], []
    for n in range(RG_BLOCKS):
        rn = rec[:, n * RG_BLOCK:(n + 1) * RG_BLOCK].astype(BF16)
        ax = jnp.dot(rn, wax_ref[n], preferred_element_type=F32)
        a_parts.append(ax[:, :RG_BLOCK])
        i_parts.append(ax[:, RG_BLOCK:])
    r = _sigmoid(jnp.concatenate(a_parts, axis=1) + ba_ref[...])
    i = _sigmoid(jnp.concatenate(i_parts, axis=1) + bx_ref[...])
    log_a = (-RG_C) * r * _softplus(-lam_ref[...])
    a = jnp.exp(log_a)
    bx = jnp.sqrt(1.0 - a * a) * (i * rec)
    return a, bx


def _group_scan(a, b):
    row = lax.broadcasted_iota(jnp.int32, a.shape, 0) & (SUBLANES - 1)
    for s in (1, 2, 4):
        a_sh = pltpu.roll(a, s, 0)
        b_sh = pltpu.roll(b, s, 0)
        m = row >= s
        b = jnp.where(m, a * b_sh + b, b)
        a = jnp.where(m, a * a_sh, a)
    return a, b


def _rg_prompt_kernel(x_ref, win_ref, cw_ref, cb_ref, wax_ref, ba_ref, bx_ref, lam_ref, wout_ref,
                      g_ref, b_ref, y_ref, ht_ref, cbuf_ref, hcar, tail, cs, a_s, b_s, h_s):
    t = pl.program_id(1)
    nt = pl.num_programs(1)
    rows = x_ref.shape[0]

    @pl.when(t == 0)
    def _():
        hcar[...] = jnp.zeros_like(hcar)
        tail[...] = jnp.zeros_like(tail)

    x = x_ref[...]
    xin = jnp.dot(x.astype(BF16), win_ref[...], preferred_element_type=F32)
    gate = _gelu(xin[:, :D])
    raw = xin[:, D:]
    cs[0:SUBLANES, :] = tail[...]
    cs[SUBLANES:SUBLANES + rows, :] = raw
    cw = cw_ref[...]
    rec = (cb_ref[...] + cw[3:4] * raw + cw[2:3] * cs[7:7 + rows, :]
           + cw[1:2] * cs[6:6 + rows, :] + cw[0:1] * cs[5:5 + rows, :])
    tail[...] = raw[rows - SUBLANES:rows, :]

    a, bx = _rg_gates(rec, wax_ref, ba_ref, bx_ref, lam_ref)
    a, bx = _group_scan(a, bx)
    a_s[...] = a
    b_s[...] = bx

    def body(g, h):
        r0 = pl.multiple_of(g * SUBLANES, SUBLANES)
        hg = a_s[pl.ds(r0, SUBLANES), :] * h + b_s[pl.ds(r0, SUBLANES), :]
        h_s[pl.ds(r0, SUBLANES), :] = hg
        return jnp.broadcast_to(hg[SUBLANES - 1:SUBLANES, :], (SUBLANES, D))

    hl = lax.fori_loop(0, rows // SUBLANES, body, hcar[...])
    hcar[...] = hl

    y = h_s[...] * gate
    mix = jnp.dot(y.astype(BF16), wout_ref[...], preferred_element_type=F32)
    y_ref[...] = _layer_norm(ALPHA * x + mix, g_ref[...], b_ref[...])

    @pl.when(t == nt - 1)
    def _():
        ht_ref[0] = hl[0:1, :]
        cbuf_ref[0] = cs[rows + 5:rows + 8, :]


def _rg_sample_kernel(x_ref, h0_ref, buf_ref, win_ref, cw_ref, cb_ref, wax_ref, ba_ref, bx_ref, lam_ref,
                      wout_ref, g_ref, b_ref, y_ref, hs_ref, raw_ref):
    rows = x_ref.shape[0]
    x = x_ref[...]
    xin = jnp.dot(x.astype(BF16), win_ref[...], preferred_element_type=F32)
    gate = _gelu(xin[:, :D])
    raw = xin[:, D:]
    tmod = lax.broadcasted_iota(jnp.int32, raw.shape, 0) & (SUBLANES - 1)
    bufv = buf_ref[...]
    cw = cw_ref[...]
    rec = cb_ref[...] + cw[3:4] * raw
    for j in (1, 2, 3):
        sh = pltpu.roll(raw, j, 0)
        bsh = pltpu.roll(bufv, rows + j - SUBLANES, 0)
        rec = rec + cw[3 - j:4 - j] * jnp.where(tmod >= j, sh, bsh)
    a, bx = _rg_gates(rec, wax_ref, ba_ref, bx_ref, lam_ref)
    a, bx = _group_scan(a, bx)
    hh = a * h0_ref[...] + bx
    mix = jnp.dot((hh * gate).astype(BF16), wout_ref[...], preferred_element_type=F32)
    y_ref[...] = _layer_norm(ALPHA * x + mix, g_ref[...], b_ref[...])
    hs_ref[...] = hh
    raw_ref[...] = raw


def _rg_layer(x_all, state_h, state_conv, w):
    weights = [w["w_in"], w["conv_w"], w["conv_b"], w["w_ax"], w["b_a"], w["b_x"], w["lam"], w["w_out"],
               w["ln_g"], w["ln_b"]]
    wspecs = [_full(a.shape) for a in weights]
    nt = T_PROMPT // RG_TILE
    x_all, ht_p, cbuf_p = pl.pallas_call(
        _rg_prompt_kernel,
        grid=(BATCH, nt),
        in_specs=[pl.BlockSpec((RG_TILE, D), lambda b, t: (b * nt + t, 0))] + wspecs,
        out_specs=[pl.BlockSpec((RG_TILE, D), lambda b, t: (b * nt + t, 0)),
                   pl.BlockSpec((1, 1, D), lambda b, t: (b, 0, 0)),
                   pl.BlockSpec((1, CONV_W - 1, D), lambda b, t: (b, 0, 0))],
        out_shape=[jax.ShapeDtypeStruct((N_TOT, D), F32),
                   jax.ShapeDtypeStruct((BATCH, 1, D), F32),
                   jax.ShapeDtypeStruct((BATCH, CONV_W - 1, D), F32)],
        scratch_shapes=[pltpu.VMEM((SUBLANES, D), F32), pltpu.VMEM((SUBLANES, D), F32),
                        pltpu.VMEM((RG_TILE + SUBLANES, D), F32), pltpu.VMEM((RG_TILE, D), F32),
                        pltpu.VMEM((RG_TILE, D), F32), pltpu.VMEM((RG_TILE, D), F32)],
        input_output_aliases={0: 0},
        compiler_params=_params(("arbitrary", "arbitrary")),
        name="rg_prompt",
    )(x_all, *weights)

    h0e = jnp.repeat(state_h, DEC_SEQ, axis=0)
    buf8 = jnp.pad(state_conv, ((0, 0), (DEC_SEQ - (CONV_W - 1), 0), (0, 0))).reshape(N_SAMPLE, D)
    tile = pl.BlockSpec((TOK_TILE, D), lambda i: (i, 0))
    xtile = pl.BlockSpec((TOK_TILE, D), lambda i: (SAMPLE_BLK0 + i, 0))
    x_all, hs, raws = pl.pallas_call(
        _rg_sample_kernel,
        grid=(N_SAMPLE // TOK_TILE,),
        in_specs=[xtile, tile, tile] + wspecs,
        out_specs=[xtile, tile, tile],
        out_shape=[jax.ShapeDtypeStruct((N_TOT, D), F32),
                   jax.ShapeDtypeStruct((N_SAMPLE, D), F32),
                   jax.ShapeDtypeStruct((N_SAMPLE, D), F32)],
        input_output_aliases={0: 0},
        compiler_params=_params(("arbitrary",)),
        name="rg_sample",
    )(x_all, h0e, buf8, *weights)
    ht_s = hs.reshape(DEC_BATCH, DEC_SEQ, D)[:, DEC_SEQ - 1]
    cbuf_s = raws.reshape(DEC_BATCH, DEC_SEQ, D)[:, DEC_SEQ - (CONV_W - 1):]
    return x_all, ht_p.reshape(BATCH, D), cbuf_p, ht_s, cbuf_s


def _extract_top(s, k, out_ref):
    for i in range(k):
        m = jnp.max(s, axis=0, keepdims=True)
        out_ref[i:i + 1, :] = m
        s = jnp.where(s == m, -jnp.inf, s)


def _router_kernel(x_ref, wqt_ref, k1_ref, k2_ref, th_ref, cc_ref, s2_ref, e2_ref, v1s, v2s, cs):
    tt = x_ref.shape[0]
    xb = x_ref[...].astype(BF16)
    qt = lax.dot_general(wqt_ref[...], xb, NT_DIMS, preferred_element_type=F32)
    neg = jnp.full((3 * SUBLANES, tt), -jnp.inf, F32)
    rowi = lax.broadcasted_iota(jnp.int32, (SUBLANES, tt), 0)
    for h in range(PEER_HEADS):
        q1 = qt[h * 256:h * 256 + 128, :].astype(BF16)
        q2 = qt[h * 256 + 128:(h + 1) * 256, :].astype(BF16)
        s1 = jnp.dot(k1_ref[h], q1, preferred_element_type=F32)
        s2 = jnp.dot(k2_ref[h], q2, preferred_element_type=F32)
        v1s[...] = neg
        v2s[...] = neg
        _extract_top(s1, N_CAND, v1s)
        _extract_top(s2, N_CAND, v2s)
        v1 = v1s[...]
        v2 = v2s[...]
        blocks = [v1 + v2[0:1]]
        for b in range(1, SUBLANES):
            c = v1[0:SUBLANES] + v2[b:b + 1]
            lim = N_CAND // (b + 1)
            if lim < SUBLANES:
                c = jnp.where(rowi < lim, c, -jnp.inf)
            blocks.append(c)
        blocks.append(v2[SUBLANES:3 * SUBLANES] + v1[0:1])
        cs[...] = neg
        _extract_top(jnp.concatenate(blocks, axis=0), N_CAND, cs)
        top = cs[0:PEER_TOPK, :]
        z = jnp.sum(jnp.exp(top - top[0:1]), axis=0, keepdims=True)
        tau = 0.5 * (cs[PEER_TOPK - 1:PEER_TOPK, :] + cs[PEER_TOPK:PEER_TOPK + 1, :])
        th_ref[h] = tau - s1
        cc_ref[h] = jnp.exp(s1 - v1[0:1]) * (1.0 / z)
        s2_ref[h] = s2
        e2_ref[h] = jnp.exp(s2 - v2[0:1])


def _expert_kernel(x_ref, u_ref, vt_ref, th_ref, cc_ref, s2_ref, e2_ref, g_ref, b_ref, y_ref,
                   xb_s, act_s, w_s, acc_s):
    j = pl.program_id(1)
    nj = pl.num_programs(1)
    tt = x_ref.shape[0]
    rows_per_step = u_ref.shape[0] // N_KEYS
    assert rows_per_step == SUBLANES

    @pl.when(j == 0)
    def _():
        xb_s[...] = x_ref[...].astype(BF16)
        acc_s[...] = jnp.zeros_like(acc_s)

    act_s[...] = lax.dot_general(u_ref[...], xb_s[...], NT_DIMS, preferred_element_type=F32)

    a0 = pl.multiple_of(j * rows_per_step, SUBLANES)

    def cbody(c, carry):
        ls = pl.ds(pl.multiple_of(c * LANES, LANES), LANES)
        thg = [th_ref[h, pl.ds(a0, SUBLANES), ls] for h in range(PEER_HEADS)]
        ccg = [cc_ref[h, pl.ds(a0, SUBLANES), ls] for h in range(PEER_HEADS)]
        for r in range(rows_per_step):
            rs = slice(r * N_KEYS, (r + 1) * N_KEYS)
            gsum = jnp.zeros((N_KEYS, LANES), F32)
            for h in range(PEER_HEADS):
                sel = jnp.where(s2_ref[h, :, ls] >= thg[h][r:r + 1], e2_ref[h, :, ls], 0.0)
                gsum = gsum + sel * ccg[h][r:r + 1]
            w_s[rs, ls] = (gsum * _gelu(act_s[rs, ls])).astype(BF16)
        return carry

    lax.fori_loop(0, tt // LANES, cbody, 0)
    acc_s[...] += jnp.dot(vt_ref[...], w_s[...], preferred_element_type=F32)

    @pl.when(j == nj - 1)
    def _():
        y_ref[...] = _layer_norm(ALPHA * x_ref[...] + acc_s[...].T, g_ref[...], b_ref[...])


def _peer_layer(x_all, w):
    nt = N_TOT // TOK_TILE
    gate_shape = jax.ShapeDtypeStruct((PEER_HEADS, N_KEYS, N_TOT), F32)
    gate_spec = pl.BlockSpec((PEER_HEADS, N_KEYS, TOK_TILE), lambda i: (0, 0, i))
    th, cc, s2, e2 = pl.pallas_call(
        _router_kernel,
        grid=(nt,),
        in_specs=[pl.BlockSpec((TOK_TILE, D), lambda i: (i, 0)),
                  _full(w["wq_t"].shape), _full(w["keys1"].shape), _full(w["keys2"].shape)],
        out_specs=[gate_spec] * 4,
        out_shape=[gate_shape] * 4,
        scratch_shapes=[pltpu.VMEM((3 * SUBLANES, TOK_TILE), F32)] * 3,
        compiler_params=_params(("arbitrary",)),
        name="peer_router",
    )(x_all, w["wq_t"], w["keys1"], w["keys2"])

    ne = N_EXPERTS // EXP_TILE
    gate_spec2 = pl.BlockSpec((PEER_HEADS, N_KEYS, TOK_TILE), lambda i, j: (0, 0, i))
    return pl.pallas_call(
        _expert_kernel,
        grid=(nt, ne),
        in_specs=[pl.BlockSpec((TOK_TILE, D), lambda i, j: (i, 0)),
                  pl.BlockSpec((EXP_TILE, D), lambda i, j: (j, 0)),
                  pl.BlockSpec((D, EXP_TILE), lambda i, j: (0, j)),
                  gate_spec2, gate_spec2, gate_spec2, gate_spec2,
                  _full((1, D)), _full((1, D))],
        out_specs=pl.BlockSpec((TOK_TILE, D), lambda i, j: (i, 0)),
        out_shape=jax.ShapeDtypeStruct((N_TOT, D), F32),
        scratch_shapes=[pltpu.VMEM((TOK_TILE, D), BF16), pltpu.VMEM((EXP_TILE, TOK_TILE), F32),
                        pltpu.VMEM((EXP_TILE, TOK_TILE), BF16), pltpu.VMEM((D, TOK_TILE), F32)],
        compiler_params=_params(("arbitrary", "arbitrary")),
        name="peer_experts",
    )(x_all, w["u"], w["v_t"], th, cc, s2, e2, w["ln_g"], w["ln_b"])


def _cumsum(x, axis):
    n = x.shape[axis]
    idx = lax.broadcasted_iota(jnp.int32, x.shape, axis)
    s = 1
    while s < n:
        x = x + jnp.where(idx >= s, pltpu.roll(x, s, axis), 0.0)
        s *= 2
    return x


def _log_forget(zf, bf):
    zf = zf + bf
    lf = jnp.minimum(zf, 0.0) - jnp.log1p(jnp.exp(-jnp.abs(zf)))
    lane = lax.broadcasted_iota(jnp.int32, lf.shape, 1)
    return jnp.where(lane < N_HEADS, lf, 0.0)


def _fox_proj_prompt_kernel(x_ref, w_ref, bf_ref, q_ref, khm_ref, vhm_ref, k_ref, v_ref, lf_ref, c_ref,
                            ccar):
    t = pl.program_id(1)
    rows = x_ref.shape[0]

    @pl.when(t == 0)
    def _():
        ccar[...] = jnp.zeros_like(ccar)

    z = jnp.dot(x_ref[...].astype(BF16), w_ref[...], preferred_element_type=F32)
    for h in range(N_HEADS):
        q_ref[0, h] = z[:, h * HEAD_DIM:(h + 1) * HEAD_DIM]
        khm_ref[0, h] = z[:, D + h * HEAD_DIM:D + (h + 1) * HEAD_DIM].astype(BF16)
        vhm_ref[0, h] = z[:, 2 * D + h * HEAD_DIM:2 * D + (h + 1) * HEAD_DIM].astype(BF16)
    k_ref[0] = z[:, D:2 * D]
    v_ref[0] = z[:, 2 * D:3 * D]
    lf = _log_forget(z[:, 3 * D:], bf_ref[...])
    lf_ref[0] = lf
    c = _cumsum(lf, 0) + ccar[0:1, :]
    c_ref[0] = c
    ccar[...] = jnp.broadcast_to(c[rows - 1:rows, :], ccar.shape)


def _fox_proj_sample_kernel(x_ref, w_ref, bf_ref, q_ref, k_ref, v_ref, lf_ref):
    z = jnp.dot(x_ref[...].astype(BF16), w_ref[...], preferred_element_type=F32)
    q_ref[...] = z[:, :D]
    k_ref[...] = z[:, D:2 * D]
    v_ref[...] = z[:, 2 * D:3 * D]
    lf_ref[...] = _log_forget(z[:, 3 * D:], bf_ref[...])


def _attn_prompt_kernel(q_ref, k_ref, v_ref, ck_ref, c_ref, o_ref):
    h = pl.program_id(1)
    qt = pl.program_id(2)
    q = q_ref[0, 0].astype(BF16)
    lane = lax.broadcasted_iota(jnp.int32, (ATT_TQ, LANES), 1)
    cq = jnp.sum(jnp.where(lane == h, c_ref[0], 0.0), axis=1, keepdims=True)
    qpos = qt * ATT_TQ + lax.broadcasted_iota(jnp.int32, (ATT_TQ, 1), 0)

    def chunk(kc, vc, ckc, kpos0, carry):
        m, l, acc = carry
        tk = kc.shape[0]
        s = lax.dot_general(q, kc, NT_DIMS, preferred_element_type=F32) * ATTN_SCALE + cq - ckc
        kpos = kpos0 + lax.broadcasted_iota(jnp.int32, (ATT_TQ, tk), 1)
        s = jnp.where(kpos <= qpos, s, -jnp.inf)
        mn = jnp.maximum(m, jnp.max(s, axis=1, keepdims=True))
        alpha = jnp.exp(m - mn)
        p = jnp.exp(s - mn)
        l = alpha * l + jnp.sum(p, axis=1, keepdims=True)
        acc = alpha * acc + jnp.dot(p.astype(BF16), vc, preferred_element_type=F32)
        return mn, l, acc

    def body(c, carry):
        k0 = pl.multiple_of(c * ATT_TK, ATT_TK)
        return chunk(k_ref[0, 0, pl.ds(k0, ATT_TK), :], v_ref[0, 0, pl.ds(k0, ATT_TK), :],
                     ck_ref[0, 0, :, pl.ds(k0, ATT_TK)], k0, carry)

    n_main = (T_PROMPT // ATT_TK) * ATT_TK
    last_q = qt * ATT_TQ + (ATT_TQ - 1)
    n_chunks = jnp.minimum(lax.shift_right_logical(last_q, 8) + 1, n_main // ATT_TK)
    carry = (jnp.full((ATT_TQ, 1), -jnp.inf, F32), jnp.zeros((ATT_TQ, 1), F32),
             jnp.zeros((ATT_TQ, HEAD_DIM), F32))
    carry = lax.fori_loop(0, n_chunks, body, carry)
    carry = chunk(k_ref[0, 0, n_main:T_PROMPT, :], v_ref[0, 0, n_main:T_PROMPT, :],
                  ck_ref[0, 0, :, n_main:T_PROMPT], n_main, carry)
    _, l, acc = carry
    o_ref[0] = acc * (1.0 / l)


def _attn_sample_kernel(pt_ref, q_ref, kc_ref, vc_ref, lf_ref, kn_ref, vn_ref, lfn_ref, o_ref,
                        qb, m_s, l_s, acc, car):
    del pt_ref
    p = pl.program_id(1)
    np_ = pl.num_programs(1)
    nrow = N_HEADS * DEC_SEQ

    @pl.when(p == 0)
    def _():
        q = q_ref[...]
        lane_blk = lax.shift_right_logical(lax.broadcasted_iota(jnp.int32, q.shape, 1), 7)
        qb[...] = jnp.concatenate([jnp.where(lane_blk == h, q, 0.0) for h in range(N_HEADS)],
                                  axis=0).astype(BF16)
        m_s[...] = jnp.full(m_s.shape, -jnp.inf, F32)
        l_s[...] = jnp.zeros_like(l_s)
        acc[...] = jnp.zeros_like(acc)
        car[...] = jnp.zeros_like(car)

    def rows_per_head(x):
        return jnp.concatenate([jnp.broadcast_to(x[h:h + 1, :], (DEC_SEQ, x.shape[1]))
                                for h in range(N_HEADS)], axis=0)

    def update(kb, vb, bias, mask):
        s = lax.dot_general(qb[...], kb, NT_DIMS, preferred_element_type=F32) * ATTN_SCALE - bias
        if mask is not None:
            s = jnp.where(mask, s, -jnp.inf)
        m_old = m_s[...]
        mn = jnp.maximum(m_old, jnp.max(s, axis=1, keepdims=True))
        alpha = jnp.exp(m_old - mn)
        pr = jnp.exp(s - mn)
        l_s[...] = alpha * l_s[...] + jnp.sum(pr, axis=1, keepdims=True)
        acc[...] = alpha * acc[...] + jnp.dot(pr.astype(BF16), vb, preferred_element_type=F32)
        m_s[...] = mn

    pre = _cumsum(lf_ref[0], 1) + car[...]
    update(kc_ref[0].astype(BF16), vc_ref[0].astype(BF16), rows_per_head(pre), None)
    car[...] = pre[:, LANES - 1:LANES]

    @pl.when(p == np_ - 1)
    def _():
        zpad = jnp.zeros((PAGE_SIZE - DEC_SEQ, D), F32)
        kb = jnp.concatenate([kn_ref[...], zpad], axis=0).astype(BF16)
        vb = jnp.concatenate([vn_ref[...], zpad], axis=0).astype(BF16)
        local = _cumsum(lfn_ref[0], 1)
        bias = rows_per_head(local + car[...])
        qi = lax.broadcasted_iota(jnp.int32, (nrow, LANES), 0) & (DEC_SEQ - 1)
        ki = lax.broadcasted_iota(jnp.int32, (nrow, LANES), 1)
        update(kb, vb, bias, ki <= qi)
        o = acc[...] * (1.0 / l_s[...])
        for h in range(N_HEADS):
            o_ref[:, h * HEAD_DIM:(h + 1) * HEAD_DIM] = (
                o[h * DEC_SEQ:(h + 1) * DEC_SEQ, h * HEAD_DIM:(h + 1) * HEAD_DIM])


def _out_proj_kernel(a_ref, x_ref, w_ref, g_ref, b_ref, y_ref):
    mix = jnp.dot(a_ref[...].astype(BF16), w_ref[...], preferred_element_type=F32)
    y_ref[...] = _layer_norm(ALPHA * x_ref[...] + mix, g_ref[...], b_ref[...])


def _out_proj(x_all, a, w_o, ln_g, ln_b, tile, blk0):
    n = a.shape[0] // tile
    xspec = pl.BlockSpec((tile, D), lambda i: (blk0 + i, 0))
    return pl.pallas_call(
        _out_proj_kernel,
        grid=(n,),
        in_specs=[pl.BlockSpec((tile, D), lambda i: (i, 0)), xspec, _full((D, D)), _full((1, D)),
                  _full((1, D))],
        out_specs=xspec,
        out_shape=jax.ShapeDtypeStruct((N_TOT, D), F32),
        input_output_aliases={1: 0},
        compiler_params=_params(("arbitrary",)),
        name="fox_out_proj",
    )(a, x_all, w_o, ln_g, ln_b)


def _fox_layer(x_all, cache_k, cache_v, cache_lf_t, page_table, w):
    nt = T_PROMPT // FOX_TILE
    hm_shape = jax.ShapeDtypeStruct((BATCH, N_HEADS, T_PROMPT, HEAD_DIM), BF16)
    hm_spec = pl.BlockSpec((1, N_HEADS, FOX_TILE, HEAD_DIM), lambda b, t: (b, 0, t, 0))
    nat_shape = jax.ShapeDtypeStruct((BATCH, T_PROMPT, D), F32)
    nat_spec = pl.BlockSpec((1, FOX_TILE, D), lambda b, t: (b, t, 0))
    lf_shape = jax.ShapeDtypeStruct((BATCH, T_PROMPT, LANES), F32)
    lf_spec = pl.BlockSpec((1, FOX_TILE, LANES), lambda b, t: (b, t, 0))
    q_hm, k_hm, v_hm, k_p, v_p, lf_p, c_p = pl.pallas_call(
        _fox_proj_prompt_kernel,
        grid=(BATCH, nt),
        in_specs=[pl.BlockSpec((FOX_TILE, D), lambda b, t: (b * nt + t, 0)),
                  _full((D, FOX_NPAD)), _full((1, LANES))],
        out_specs=[hm_spec, hm_spec, hm_spec, nat_spec, nat_spec, lf_spec, lf_spec],
        out_shape=[jax.ShapeDtypeStruct(hm_shape.shape, F32), hm_shape, hm_shape, nat_shape, nat_shape,
                   lf_shape, lf_shape],
        scratch_shapes=[pltpu.VMEM((SUBLANES, LANES), F32)],
        compiler_params=_params(("arbitrary", "arbitrary")),
        name="fox_proj_prompt",
    )(x_all, w["w_in"], w["b_f"])

    ck_row = jnp.pad(jnp.transpose(c_p[:, :, :N_HEADS], (0, 2, 1)),
                     ((0, 0), (0, 0), (0, T_KPAD - T_PROMPT)))[:, :, None, :]
    nq = T_PROMPT // ATT_TQ
    kv_spec = pl.BlockSpec((1, 1, T_PROMPT, HEAD_DIM), lambda b, h, q: (b, h, 0, 0))
    attn_p = pl.pallas_call(
        _attn_prompt_kernel,
        grid=(BATCH, N_HEADS, nq),
        in_specs=[pl.BlockSpec((1, 1, ATT_TQ, HEAD_DIM), lambda b, h, q: (b, h, q, 0)),
                  kv_spec, kv_spec,
                  pl.BlockSpec((1, 1, 1, T_KPAD), lambda b, h, q: (b, h, 0, 0)),
                  pl.BlockSpec((1, ATT_TQ, LANES), lambda b, h, q: (b, q, 0))],
        out_specs=pl.BlockSpec((1, ATT_TQ, HEAD_DIM), lambda b, h, q: (b, q, h)),
        out_shape=jax.ShapeDtypeStruct((BATCH, T_PROMPT, D), F32),
        compiler_params=_params(("arbitrary", "arbitrary", "arbitrary")),
        name="fox_attn_prompt",
    )(q_hm, k_hm, v_hm, ck_row, c_p)

    ns = N_SAMPLE // TOK_TILE
    tile = pl.BlockSpec((TOK_TILE, D), lambda i: (i, 0))
    q_s, k_s, v_s, lf_s = pl.pallas_call(
        _fox_proj_sample_kernel,
        grid=(ns,),
        in_specs=[pl.BlockSpec((TOK_TILE, D), lambda i: (SAMPLE_BLK0 + i, 0)),
                  _full((D, FOX_NPAD)), _full((1, LANES))],
        out_specs=[tile, tile, tile, pl.BlockSpec((TOK_TILE, LANES), lambda i: (i, 0))],
        out_shape=[jax.ShapeDtypeStruct((N_SAMPLE, D), F32), jax.ShapeDtypeStruct((N_SAMPLE, D), F32),
                   jax.ShapeDtypeStruct((N_SAMPLE, D), F32), jax.ShapeDtypeStruct((N_SAMPLE, LANES), F32)],
        compiler_params=_params(("arbitrary",)),
        name="fox_proj_sample",
    )(x_all, w["w_in"], w["b_f"])

    lfn = jnp.transpose(lf_s[:, :N_HEADS].reshape(DEC_BATCH, DEC_SEQ, N_HEADS), (0, 2, 1))
    lfn = jnp.pad(lfn, ((0, 0), (0, 0), (0, LANES - DEC_SEQ)))
    seq_spec = pl.BlockSpec((DEC_SEQ, D), lambda s, p, pt: (s, 0))
    page_spec = pl.BlockSpec((1, PAGE_SIZE, D), lambda s, p, pt: (pt[s, p], 0, 0))
    nrow = N_HEADS * DEC_SEQ
    attn_s = pl.pallas_call(
        _attn_sample_kernel,
        grid_spec=pltpu.PrefetchScalarGridSpec(
            num_scalar_prefetch=1,
            grid=(DEC_BATCH, N_PAGES),
            in_specs=[seq_spec, page_spec, page_spec,
                      pl.BlockSpec((1, N_HEADS, PAGE_SIZE), lambda s, p, pt: (pt[s, p], 0, 0)),
                      seq_spec, seq_spec,
                      pl.BlockSpec((1, N_HEADS, LANES), lambda s, p, pt: (s, 0, 0))],
            out_specs=seq_spec,
            scratch_shapes=[pltpu.VMEM((nrow, D), BF16), pltpu.VMEM((nrow, 1), F32),
                            pltpu.VMEM((nrow, 1), F32), pltpu.VMEM((nrow, D), F32),
                            pltpu.VMEM((N_HEADS, 1), F32)]),
        out_shape=jax.ShapeDtypeStruct((N_SAMPLE, D), F32),
        compiler_params=_params(("arbitrary", "arbitrary")),
        name="fox_attn_sample",
    )(page_table, q_s, cache_k, cache_v, cache_lf_t, k_s, v_s, lfn)

    x_all = _out_proj(x_all, attn_p.reshape(N_PROMPT, D), w["w_o"], w["ln_g"], w["ln_b"], FOX_TILE, 0)
    x_all = _out_proj(x_all, attn_s, w["w_o"], w["ln_g"], w["ln_b"], TOK_TILE, SAMPLE_BLK0)
    return x_all, k_p, v_p, lf_p[:, :, :N_HEADS], k_s, v_s, lf_s[:, :N_HEADS]


def _row(v):
    return v.reshape(1, -1).astype(F32)


def kernel(x_prompt, x_sample, cache_k, cache_v, cache_logf, state_h, state_conv, page_table, meta_tokens,
           rg_w_in, rg_conv_w, rg_conv_b, rg_w_a, rg_b_a, rg_w_x, rg_b_x, rg_lambda, rg_w_out, fox_w_in,
           fox_b_f, fox_w_o, peer_w_q, peer_keys1, peer_keys2, peer_u, peer_v, ln_mix_g, ln_mix_b,
           ln_ffn_g, ln_ffn_b):
    meta = jnp.broadcast_to(meta_tokens.astype(F32)[None], (BATCH, N_META, D))
    hp = jnp.concatenate([meta, x_prompt], axis=1).reshape(N_PROMPT, D)
    x_all = jnp.concatenate([hp, jnp.zeros((SAMPLE_OFF - N_PROMPT, D), F32),
                             x_sample.reshape(N_SAMPLE, D)], axis=0)

    def peer_weights(i):
        return {"wq_t": peer_w_q[i].T.astype(BF16), "keys1": peer_keys1[i].astype(BF16),
                "keys2": peer_keys2[i].astype(BF16), "u": peer_u[i].astype(BF16),
                "v_t": peer_v[i].T.astype(BF16), "ln_g": _row(ln_ffn_g[i]), "ln_b": _row(ln_ffn_b[i])}

    rg = {"w_in": rg_w_in[0].astype(BF16), "conv_w": rg_conv_w[0].astype(F32), "conv_b": _row(rg_conv_b[0]),
          "w_ax": jnp.concatenate([rg_w_a[0], rg_w_x[0]], axis=-1).astype(BF16),
          "b_a": _row(rg_b_a[0]), "b_x": _row(rg_b_x[0]), "lam": _row(rg_lambda[0]),
          "w_out": rg_w_out[0].astype(BF16), "ln_g": _row(ln_mix_g[0]), "ln_b": _row(ln_mix_b[0])}
    x_all, ht_p, cbuf_p, ht_s, cbuf_s = _rg_layer(x_all, state_h[0], state_conv[0], rg)
    x_all = _peer_layer(x_all, peer_weights(0))

    n_pool = cache_k.shape[1]
    fox = {"w_in": jnp.pad(fox_w_in[0], ((0, 0), (0, FOX_NPAD - fox_w_in.shape[-1]))).astype(BF16),
           "b_f": jnp.pad(fox_b_f[0], (0, LANES - N_HEADS)).reshape(1, LANES).astype(F32),
           "w_o": fox_w_o[0].astype(BF16), "ln_g": _row(ln_mix_g[1]), "ln_b": _row(ln_mix_b[1])}
    x_all, k_p, v_p, lf_p, k_s, v_s, lf_s = _fox_layer(
        x_all, cache_k.reshape(n_pool, PAGE_SIZE, D), cache_v.reshape(n_pool, PAGE_SIZE, D),
        jnp.transpose(cache_logf[0], (0, 2, 1)), page_table, fox)
    x_all = _peer_layer(x_all, peer_weights(1))

    y_prompt = x_all[:N_PROMPT].reshape(BATCH, T_PROMPT, D)[:, N_META:]
    y_sample = x_all[SAMPLE_OFF:].reshape(DEC_BATCH, DEC_SEQ, D)
    hd = (N_HEADS, HEAD_DIM)
    return (y_prompt, y_sample,
            ht_p[None], cbuf_p[None],
            k_p.reshape(1, BATCH, T_PROMPT, *hd), v_p.reshape(1, BATCH, T_PROMPT, *hd), lf_p[None],
            ht_s[None], cbuf_s[None],
            k_s.reshape(1, DEC_BATCH, DEC_SEQ, *hd), v_s.reshape(1, DEC_BATCH, DEC_SEQ, *hd),
            lf_s.reshape(1, DEC_BATCH, DEC_SEQ, N_HEADS))
```

```python
import functools

import jax
import jax.numpy as jnp
from jax import lax
from jax.experimental import pallas as pl
from jax.experimental.pallas import tpu as pltpu

F32 = jnp.float32
BF16 = jnp.bfloat16

D = 1024
BATCH = 8
N_META = 16
T_PROMPT = 2064
DEC_BATCH = 128
DEC_SEQ = 8
N_HEADS = 8
HEAD_DIM = 128
PAGE_SIZE = 128
N_PAGES = 16
RG_BLOCKS = 8
RG_BLOCK = 128
RG_C = 8.0
CONV_W = 4
PEER_HEADS = 8
N_KEYS = 128
N_EXPERTS = N_KEYS * N_KEYS
PEER_TOPK = 16
DEPTH = 2
ALPHA = (2.0 * DEPTH) ** 0.25
LN_EPS = 1e-5
ATTN_SCALE = HEAD_DIM ** -0.5

SUBLANES = 8
LANES = 128
VMEM_LIMIT = 56 * 1024 * 1024

N_PROMPT = BATCH * T_PROMPT
TOK_TILE = 512
SAMPLE_OFF = 16896
N_SAMPLE = DEC_BATCH * DEC_SEQ
N_TOT = SAMPLE_OFF + N_SAMPLE
SAMPLE_BLK0 = SAMPLE_OFF // TOK_TILE

RG_TILE = 344
FOX_TILE = 688
ATT_TQ = 344
ATT_TK = 256
ATT_HPB = 4
ATT_PPS = 4
T_KPAD = 2304
EXP_TILE = 1024
N_CAND = 17
FOX_NPAD = 3 * D + LANES

NT_DIMS = (((1,), (1,)), ((), ()))


GELU_C1 = 0.7978845608028654
GELU_C3 = GELU_C1 * 0.044715


def _gelu(x):
    return 0.5 * x * (1.0 + jnp.tanh(x * (GELU_C1 + GELU_C3 * (x * x))))


def _sigmoid(x):
    return 1.0 / (1.0 + jnp.exp(-x))


def _softplus(x):
    return jnp.maximum(x, 0.0) + jnp.log1p(jnp.exp(-jnp.abs(x)))


def _layer_norm(x, g, b):
    mu = jnp.mean(x, axis=-1, keepdims=True)
    xc = x - mu
    var = jnp.mean(xc * xc, axis=-1, keepdims=True)
    return xc * lax.rsqrt(var + LN_EPS) * g + b


def _full(shape):
    n = len(shape)
    return pl.BlockSpec(shape, lambda *_: (0,) * n)


def _params(sem, **kw):
    return pltpu.CompilerParams(dimension_semantics=sem, vmem_limit_bytes=VMEM_LIMIT, **kw)


def _rg_gates(rec, wax_ref, ba_ref, bx_ref, lam_ref):
    a_parts, i_parts = [], []
    for n in range(RG_BLOCKS):
        rn = rec[:, n * RG_BLOCK:(n + 1) * RG_BLOCK].astype(BF16)
        ax = jnp.dot(rn, wax_ref[n], preferred_element_type=F32)
        a_parts.append(ax[:, :RG_BLOCK])
        i_parts.append(ax[:, RG_BLOCK:])
    r = _sigmoid(jnp.concatenate(a_parts, axis=1) + ba_ref[...])
    i = _sigmoid(jnp.concatenate(i_parts, axis=1) + bx_ref[...])
    log_a = (-RG_C) * r * _softplus(-lam_ref[...])
    a = jnp.exp(log_a)
    bx = jnp.sqrt(1.0 - a * a) * (i * rec)
    return a, bx


def _group_scan(a, b):
    row = lax.broadcasted_iota(jnp.int32, a.shape, 0) & (SUBLANES - 1)
    for s in (1, 2, 4):
        a_sh = pltpu.roll(a, s, 0)
        b_sh = pltpu.roll(b, s, 0)
        m = row >= s
        b = jnp.where(m, a * b_sh + b, b)
        a = jnp.where(m, a * a_sh, a)
    return a, b


def _rg_prompt_kernel(x_ref, win_ref, cw_ref, cb_ref, wax_ref, ba_ref, bx_ref, lam_ref, wout_ref,
                      g_ref, b_ref, y_ref, ht_ref, cbuf_ref, hcar, tail, cs, a_s, b_s, h_s):
    t = pl.program_id(1)
    nt = pl.num_programs(1)
    rows = x_ref.shape[0]

    @pl.when(t == 0)
    def _():
        hcar[...] = jnp.zeros_like(hcar)
        tail[...] = jnp.zeros_like(tail)

    x = x_ref[...]
    xin = jnp.dot(x.astype(BF16), win_ref[...], preferred_element_type=F32)
    gate = _gelu(xin[:, :D])
    raw = xin[:, D:]
    cs[0:SUBLANES, :] = tail[...]
    cs[SUBLANES:SUBLANES + rows, :] = raw
    cw = cw_ref[...]
    rec = (cb_ref[...] + cw[3:4] * raw + cw[2:3] * cs[7:7 + rows, :]
           + cw[1:2] * cs[6:6 + rows, :] + cw[0:1] * cs[5:5 + rows, :])
    tail[...] = raw[rows - SUBLANES:rows, :]

    a, bx = _rg_gates(rec, wax_ref, ba_ref, bx_ref, lam_ref)
    a, bx = _group_scan(a, bx)
    a_s[...] = a
    b_s[...] = bx

    def body(g, h):
        r0 = pl.multiple_of(g * SUBLANES, SUBLANES)
        hg = a_s[pl.ds(r0, SUBLANES), :] * h + b_s[pl.ds(r0, SUBLANES), :]
        h_s[pl.ds(r0, SUBLANES), :] = hg
        return jnp.broadcast_to(hg[SUBLANES - 1:SUBLANES, :], (SUBLANES, D))

    hl = lax.fori_loop(0, rows // SUBLANES, body, hcar[...])
    hcar[...] = hl

    y = h_s[...] * gate
    mix = jnp.dot(y.astype(BF16), wout_ref[...], preferred_element_type=F32)
    y_ref[...] = _layer_norm(ALPHA * x + mix, g_ref[...], b_ref[...])

    @pl.when(t == nt - 1)
    def _():
        ht_ref[0] = hl[0:1, :]
        cbuf_ref[0] = cs[rows + 5:rows + 8, :]


def _rg_sample_kernel(x_ref, h0_ref, buf_ref, win_ref, cw_ref, cb_ref, wax_ref, ba_ref, bx_ref, lam_ref,
                      wout_ref, g_ref, b_ref, y_ref, hs_ref, raw_ref):
    rows = x_ref.shape[0]
    x = x_ref[...]
    xin = jnp.dot(x.astype(BF16), win_ref[...], preferred_element_type=F32)
    gate = _gelu(xin[:, :D])
    raw = xin[:, D:]
    tmod = lax.broadcasted_iota(jnp.int32, raw.shape, 0) & (SUBLANES - 1)
    bufv = buf_ref[...]
    cw = cw_ref[...]
    rec = cb_ref[...] + cw[3:4] * raw
    for j in (1, 2, 3):
        sh = pltpu.roll(raw, j, 0)
        bsh = pltpu.roll(bufv, rows + j - SUBLANES, 0)
        rec = rec + cw[3 - j:4 - j] * jnp.where(tmod >= j, sh, bsh)
    a, bx = _rg_gates(rec, wax_ref, ba_ref, bx_ref, lam_ref)
    a, bx = _group_scan(a, bx)
    hh = a * h0_ref[...] + bx
    mix = jnp.dot((hh * gate).astype(BF16), wout_ref[...], preferred_element_type=F32)
    y_ref[...] = _layer_norm(ALPHA * x + mix, g_ref[...], b_ref[...])
    hs_ref[...] = hh
    raw_ref[...] = raw


def _rg_layer(x_all, state_h, state_conv, w):
    weights = [w["w_in"], w["conv_w"], w["conv_b"], w["w_ax"], w["b_a"], w["b_x"], w["lam"], w["w_out"],
               w["ln_g"], w["ln_b"]]
    wspecs = [_full(a.shape) for a in weights]
    nt = T_PROMPT // RG_TILE
    x_all, ht_p, cbuf_p = pl.pallas_call(
        _rg_prompt_kernel,
        grid=(BATCH, nt),
        in_specs=[pl.BlockSpec((RG_TILE, D), lambda b, t: (b * nt + t, 0))] + wspecs,
        out_specs=[pl.BlockSpec((RG_TILE, D), lambda b, t: (b * nt + t, 0)),
                   pl.BlockSpec((1, 1, D), lambda b, t: (b, 0, 0)),
                   pl.BlockSpec((1, CONV_W - 1, D), lambda b, t: (b, 0, 0))],
        out_shape=[jax.ShapeDtypeStruct((N_TOT, D), F32),
                   jax.ShapeDtypeStruct((BATCH, 1, D), F32),
                   jax.ShapeDtypeStruct((BATCH, CONV_W - 1, D), F32)],
        scratch_shapes=[pltpu.VMEM((SUBLANES, D), F32), pltpu.VMEM((SUBLANES, D), F32),
                        pltpu.VMEM((RG_TILE + SUBLANES, D), F32), pltpu.VMEM((RG_TILE, D), F32),
                        pltpu.VMEM((RG_TILE, D), F32), pltpu.VMEM((RG_TILE, D), F32)],
        input_output_aliases={0: 0},
        compiler_params=_params(("arbitrary", "arbitrary")),
        name="rg_prompt",
    )(x_all, *weights)

    h0e = jnp.repeat(state_h, DEC_SEQ, axis=0)
    buf8 = jnp.pad(state_conv, ((0, 0), (DEC_SEQ - (CONV_W - 1), 0), (0, 0))).reshape(N_SAMPLE, D)
    tile = pl.BlockSpec((TOK_TILE, D), lambda i: (i, 0))
    xtile = pl.BlockSpec((TOK_TILE, D), lambda i: (SAMPLE_BLK0 + i, 0))
    x_all, hs, raws = pl.pallas_call(
        _rg_sample_kernel,
        grid=(N_SAMPLE // TOK_TILE,),
        in_specs=[xtile, tile, tile] + wspecs,
        out_specs=[xtile, tile, tile],
        out_shape=[jax.ShapeDtypeStruct((N_TOT, D), F32),
                   jax.ShapeDtypeStruct((N_SAMPLE, D), F32),
                   jax.ShapeDtypeStruct((N_SAMPLE, D), F32)],
        input_output_aliases={0: 0},
        compiler_params=_params(("arbitrary",)),
        name="rg_sample",
    )(x_all, h0e, buf8, *weights)
    ht_s = hs.reshape(DEC_BATCH, DEC_SEQ, D)[:, DEC_SEQ - 1]
    cbuf_s = raws.reshape(DEC_BATCH, DEC_SEQ, D)[:, DEC_SEQ - (CONV_W - 1):]
    return x_all, ht_p.reshape(BATCH, D), cbuf_p, ht_s, cbuf_s


def _extract_top(s, k, out_ref):
    for i in range(k):
        m = jnp.max(s, axis=0, keepdims=True)
        out_ref[i:i + 1, :] = m
        s = jnp.where(s == m, -jnp.inf, s)


def _router_kernel(x_ref, wqt_ref, k1_ref, k2_ref, th_ref, cc_ref, s2_ref, e2_ref, v1s, v2s, cs):
    tt = x_ref.shape[0]
    xb = x_ref[...].astype(BF16)
    qt = lax.dot_general(wqt_ref[...], xb, NT_DIMS, preferred_element_type=F32)
    neg = jnp.full((3 * SUBLANES, tt), -jnp.inf, F32)
    rowi = lax.broadcasted_iota(jnp.int32, (SUBLANES, tt), 0)
    for h in range(PEER_HEADS):
        q1 = qt[h * 256:h * 256 + 128, :].astype(BF16)
        q2 = qt[h * 256 + 128:(h + 1) * 256, :].astype(BF16)
        s1 = jnp.dot(k1_ref[h], q1, preferred_element_type=F32)
        s2 = jnp.dot(k2_ref[h], q2, preferred_element_type=F32)
        v1s[...] = neg
        v2s[...] = neg
        _extract_top(s1, N_CAND, v1s)
        _extract_top(s2, N_CAND, v2s)
        v1 = v1s[...]
        v2 = v2s[...]
        blocks = [v1 + v2[0:1]]
        for b in range(1, SUBLANES):
            c = v1[0:SUBLANES] + v2[b:b + 1]
            lim = N_CAND // (b + 1)
            if lim < SUBLANES:
                c = jnp.where(rowi < lim, c, -jnp.inf)
            blocks.append(c)
        blocks.append(v2[SUBLANES:3 * SUBLANES] + v1[0:1])
        cs[...] = neg
        _extract_top(jnp.concatenate(blocks, axis=0), N_CAND, cs)
        top = cs[0:PEER_TOPK, :]
        z = jnp.sum(jnp.exp(top - top[0:1]), axis=0, keepdims=True)
        tau = 0.5 * (cs[PEER_TOPK - 1:PEER_TOPK, :] + cs[PEER_TOPK:PEER_TOPK + 1, :])
        th = tau - s1
        cc = jnp.exp(s1 - v1[0:1]) * (0.5 / z)
        e2 = jnp.exp(s2 - v2[0:1])
        for c in range(tt // LANES):
            ls = slice(c * LANES, (c + 1) * LANES)
            th_ref[h, c] = th[:, ls]
            cc_ref[h, c] = cc[:, ls]
            s2_ref[h, c] = s2[:, ls]
            e2_ref[h, c] = e2[:, ls]


def _expert_kernel(x_ref, u_ref, vt_ref, th_ref, cc_ref, s2_ref, e2_ref, g_ref, b_ref, y_ref,
                   xb_s, acc_s, g_s):
    j = pl.program_id(1)
    nj = pl.num_programs(1)
    tt = x_ref.shape[0]
    rows_per_step = u_ref.shape[0] // N_KEYS
    assert rows_per_step == SUBLANES

    @pl.when(j == 0)
    def _():
        xb_s[...] = x_ref[...].astype(BF16)
        acc_s[...] = jnp.zeros_like(acc_s)

    a0 = pl.multiple_of(j * rows_per_step, SUBLANES)
    nchunk = tt // LANES
    for c in range(nchunk):
        thg = [th_ref[h, c, pl.ds(a0, SUBLANES), :] for h in range(PEER_HEADS)]
        ccg = [cc_ref[h, c, pl.ds(a0, SUBLANES), :] for h in range(PEER_HEADS)]
        for r in range(rows_per_step):
            gsum = jnp.zeros((N_KEYS, LANES), F32)
            for h in range(PEER_HEADS):
                sel = jnp.where(s2_ref[h, c] >= thg[h][r:r + 1], e2_ref[h, c], 0.0)
                gsum = gsum + sel * ccg[h][r:r + 1]
            g_s[r, c] = gsum

    act = lax.dot_general(u_ref[...], xb_s[...], NT_DIMS, preferred_element_type=F32)

    group = 2
    upd = None
    for r0 in range(0, rows_per_step, group):
        w_rows = []
        for r in range(r0, r0 + group):
            row = []
            for c in range(nchunk):
                a_rc = act[r * N_KEYS:(r + 1) * N_KEYS, c * LANES:(c + 1) * LANES]
                ga = g_s[r, c] * a_rc
                th = jnp.tanh(a_rc * (GELU_C1 + GELU_C3 * (a_rc * a_rc)))
                row.append((ga + ga * th).astype(BF16))
            w_rows.append(jnp.concatenate(row, axis=1))
        w = jnp.concatenate(w_rows, axis=0)
        part = jnp.dot(vt_ref[:, r0 * N_KEYS:(r0 + group) * N_KEYS], w, preferred_element_type=F32)
        upd = part if upd is None else upd + part
    acc_s[...] += upd

    @pl.when(j == nj - 1)
    def _():
        y_ref[...] = _layer_norm(ALPHA * x_ref[...] + acc_s[...].T, g_ref[...], b_ref[...])


def _peer_layer(x_all, w):
    nt = N_TOT // TOK_TILE
    cpt = TOK_TILE // LANES
    gate_shape = jax.ShapeDtypeStruct((PEER_HEADS, N_TOT // LANES, N_KEYS, LANES), F32)
    gate_spec = pl.BlockSpec((PEER_HEADS, cpt, N_KEYS, LANES), lambda i: (0, i, 0, 0))
    th, cc, s2, e2 = pl.pallas_call(
        _router_kernel,
        grid=(nt,),
        in_specs=[pl.BlockSpec((TOK_TILE, D), lambda i: (i, 0)),
                  _full(w["wq_t"].shape), _full(w["keys1"].shape), _full(w["keys2"].shape)],
        out_specs=[gate_spec] * 4,
        out_shape=[gate_shape] * 4,
        scratch_shapes=[pltpu.VMEM((3 * SUBLANES, TOK_TILE), F32)] * 3,
        compiler_params=_params(("arbitrary",)),
        name="peer_router",
    )(x_all, w["wq_t"], w["keys1"], w["keys2"])

    ne = N_EXPERTS // EXP_TILE
    gate_spec2 = pl.BlockSpec((PEER_HEADS, cpt, N_KEYS, LANES), lambda i, j: (0, i, 0, 0))
    return pl.pallas_call(
        _expert_kernel,
        grid=(nt, ne),
        in_specs=[pl.BlockSpec((TOK_TILE, D), lambda i, j: (i, 0)),
                  pl.BlockSpec((EXP_TILE, D), lambda i, j: (j, 0)),
                  pl.BlockSpec((D, EXP_TILE), lambda i, j: (0, j)),
                  gate_spec2, gate_spec2, gate_spec2, gate_spec2,
                  _full((1, D)), _full((1, D))],
        out_specs=pl.BlockSpec((TOK_TILE, D), lambda i, j: (i, 0)),
        out_shape=jax.ShapeDtypeStruct((N_TOT, D), F32),
        scratch_shapes=[pltpu.VMEM((TOK_TILE, D), BF16), pltpu.VMEM((D, TOK_TILE), F32),
                        pltpu.VMEM((EXP_TILE // N_KEYS, cpt, N_KEYS, LANES), F32)],
        compiler_params=_params(("arbitrary", "arbitrary")),
        name="peer_experts",
    )(x_all, w["u"], w["v_t"], th, cc, s2, e2, w["ln_g"], w["ln_b"])


def _cumsum(x, axis, stride=1):
    n = x.shape[axis]
    idx = lax.broadcasted_iota(jnp.int32, x.shape, axis)
    s = stride
    while s < n:
        x = x + jnp.where(idx >= s, pltpu.roll(x, s, axis), 0.0)
        s *= 2
    return x


def _log_forget(zf, bf):
    zf = zf + bf
    lf = jnp.minimum(zf, 0.0) - jnp.log1p(jnp.exp(-jnp.abs(zf)))
    lane = lax.broadcasted_iota(jnp.int32, lf.shape, 1)
    return jnp.where(lane < N_HEADS, lf, 0.0)


def _fox_proj_prompt_kernel(x_ref, w_ref, bf_ref, q_ref, khm_ref, vhm_ref, k_ref, v_ref, lf_ref, c_ref,
                            ccar):
    t = pl.program_id(1)
    rows = x_ref.shape[0]

    @pl.when(t == 0)
    def _():
        ccar[...] = jnp.zeros_like(ccar)

    z = jnp.dot(x_ref[...].astype(BF16), w_ref[...], preferred_element_type=F32)
    for h in range(N_HEADS):
        q_ref[0, h] = z[:, h * HEAD_DIM:(h + 1) * HEAD_DIM]
        khm_ref[0, h] = z[:, D + h * HEAD_DIM:D + (h + 1) * HEAD_DIM].astype(BF16)
        vhm_ref[0, h] = z[:, 2 * D + h * HEAD_DIM:2 * D + (h + 1) * HEAD_DIM].astype(BF16)
    k_ref[0] = z[:, D:2 * D]
    v_ref[0] = z[:, 2 * D:3 * D]
    lf = _log_forget(z[:, 3 * D:], bf_ref[...])
    lf_ref[0] = lf
    c = _cumsum(lf, 0) + ccar[0:1, :]
    c_ref[0] = c
    ccar[...] = jnp.broadcast_to(c[rows - 1:rows, :], ccar.shape)


def _fox_proj_sample_kernel(x_ref, w_ref, bf_ref, q_ref, k_ref, v_ref, lf_ref):
    z = jnp.dot(x_ref[...].astype(BF16), w_ref[...], preferred_element_type=F32)
    q_ref[...] = z[:, :D]
    k_ref[...] = z[:, D:2 * D]
    v_ref[...] = z[:, 2 * D:3 * D]
    lf_ref[...] = _log_forget(z[:, 3 * D:], bf_ref[...])


def _attn_prompt_kernel(q_ref, k_ref, v_ref, ck_ref, c_ref, o_ref):
    hg = pl.program_id(1)
    qt = pl.program_id(2)
    lane = lax.broadcasted_iota(jnp.int32, (ATT_TQ, LANES), 1)
    c_blk = c_ref[0]
    qs = [q_ref[0, i].astype(BF16) for i in range(ATT_HPB)]
    cqs = [jnp.sum(jnp.where(lane == hg * ATT_HPB + i, c_blk, 0.0), axis=1, keepdims=True)
           for i in range(ATT_HPB)]
    qpos = qt * ATT_TQ + lax.broadcasted_iota(jnp.int32, (ATT_TQ, 1), 0)

    def chunk(rows, tk, kpos0, carry):
        kpos = kpos0 + lax.broadcasted_iota(jnp.int32, (ATT_TQ, tk), 1)
        visible = kpos <= qpos
        out = []
        for i in range(ATT_HPB):
            m, l, acc = carry[3 * i:3 * i + 3]
            s = lax.dot_general(qs[i], k_ref[0, i, rows, :], NT_DIMS, preferred_element_type=F32)
            s = jnp.where(visible, s * ATTN_SCALE + cqs[i] - ck_ref[0, i, :, rows], -jnp.inf)
            mn = jnp.maximum(m, jnp.max(s, axis=1, keepdims=True))
            alpha = jnp.exp(m - mn)
            p = jnp.exp(s - mn)
            l = alpha * l + jnp.sum(p, axis=1, keepdims=True)
            acc = alpha * acc + jnp.dot(p.astype(BF16), v_ref[0, i, rows, :], preferred_element_type=F32)
            out += [mn, l, acc]
        return tuple(out)

    def body(c, carry):
        k0 = pl.multiple_of(c * ATT_TK, ATT_TK)
        return chunk(pl.ds(k0, ATT_TK), ATT_TK, k0, carry)

    n_main = (T_PROMPT // ATT_TK) * ATT_TK
    last_q = qt * ATT_TQ + (ATT_TQ - 1)
    n_chunks = jnp.minimum(lax.shift_right_logical(last_q, 8) + 1, n_main // ATT_TK)
    carry = (jnp.full((ATT_TQ, 1), -jnp.inf, F32), jnp.zeros((ATT_TQ, 1), F32),
             jnp.zeros((ATT_TQ, HEAD_DIM), F32)) * ATT_HPB
    carry = lax.fori_loop(0, n_chunks, body, carry)
    carry = chunk(slice(n_main, T_PROMPT), T_PROMPT - n_main, n_main, carry)
    for i in range(ATT_HPB):
        o_ref[0, :, i * HEAD_DIM:(i + 1) * HEAD_DIM] = carry[3 * i + 2] * (1.0 / carry[3 * i + 1])


def _attn_sample_kernel(pt_ref, q_ref, *refs):
    del pt_ref
    kc_refs = refs[0:ATT_PPS]
    vc_refs = refs[ATT_PPS:2 * ATT_PPS]
    lf_refs = refs[2 * ATT_PPS:3 * ATT_PPS]
    kn_ref, vn_ref, lfn_ref, o_ref, qb, m_s, l_s, acc, car = refs[3 * ATT_PPS:]
    p = pl.program_id(1)
    np_ = pl.num_programs(1)
    nrow = DEC_SEQ * N_HEADS
    ncol = PAGE_SIZE * N_HEADS

    @pl.when(p == 0)
    def _():
        qb[...] = q_ref[0].astype(BF16)
        m_s[...] = jnp.full(m_s.shape, -jnp.inf, F32)
        l_s[...] = jnp.zeros_like(l_s)
        acc[...] = jnp.zeros_like(acc)
        car[...] = jnp.zeros_like(car)

    def update(kbs, vbs, biases, mask):
        q = qb[...]
        ss = [jnp.where(mask, lax.dot_general(q, kb, NT_DIMS, preferred_element_type=F32) * ATTN_SCALE - bias,
                        -jnp.inf) for kb, bias in zip(kbs, biases)]
        m_old = m_s[...]
        mn = m_old
        for sv in ss:
            mn = jnp.maximum(mn, jnp.max(sv, axis=1, keepdims=True))
        alpha = jnp.exp(m_old - mn)
        lsum = alpha * l_s[...]
        av = alpha * acc[...]
        for sv, vb in zip(ss, vbs):
            pr = jnp.exp(sv - mn)
            lsum = lsum + jnp.sum(pr, axis=1, keepdims=True)
            av = av + jnp.dot(pr.astype(BF16), vb, preferred_element_type=F32)
        l_s[...] = lsum
        acc[...] = av
        m_s[...] = mn

    def same_head(n):
        r = lax.broadcasted_iota(jnp.int32, (nrow, n), 0)
        c = lax.broadcasted_iota(jnp.int32, (nrow, n), 1)
        return (r & (N_HEADS - 1)) == (c & (N_HEADS - 1)), r, c

    run = car[...]
    biases = []
    for lf_ref in lf_refs:
        lf = lf_ref[0]
        biases.append(_cumsum(lf, 1, N_HEADS) + run)
        tot = lf
        sft = N_HEADS
        while sft < ncol:
            tot = tot + pltpu.roll(tot, sft, 1)
            sft *= 2
        run = run + tot
    car[...] = run
    mask, _, _ = same_head(ncol)
    update([r[0].astype(BF16) for r in kc_refs], [r[0].astype(BF16) for r in vc_refs], biases, mask)

    @pl.when(p == np_ - 1)
    def _():
        zpad = jnp.zeros((LANES - nrow, HEAD_DIM), F32)
        kb = jnp.concatenate([kn_ref[0], zpad], axis=0).astype(BF16)
        vb = jnp.concatenate([vn_ref[0], zpad], axis=0).astype(BF16)
        bias = _cumsum(lfn_ref[0], 1, N_HEADS) + car[:, 0:LANES]
        head_ok, r, c = same_head(LANES)
        causal = lax.shift_right_logical(c, 3) <= lax.shift_right_logical(r, 3)
        update([kb], [vb], [bias], head_ok & causal & (c < nrow))
        o_ref[0] = acc[...] * (1.0 / l_s[...])


def _out_proj_kernel(a_ref, x_ref, w_ref, g_ref, b_ref, y_ref):
    mix = jnp.dot(a_ref[...].astype(BF16), w_ref[...], preferred_element_type=F32)
    y_ref[...] = _layer_norm(ALPHA * x_ref[...] + mix, g_ref[...], b_ref[...])


def _out_proj(x_all, a, w_o, ln_g, ln_b, tile, blk0):
    n = a.shape[0] // tile
    xspec = pl.BlockSpec((tile, D), lambda i: (blk0 + i, 0))
    return pl.pallas_call(
        _out_proj_kernel,
        grid=(n,),
        in_specs=[pl.BlockSpec((tile, D), lambda i: (i, 0)), xspec, _full((D, D)), _full((1, D)),
                  _full((1, D))],
        out_specs=xspec,
        out_shape=jax.ShapeDtypeStruct((N_TOT, D), F32),
        input_output_aliases={1: 0},
        compiler_params=_params(("arbitrary",)),
        name="fox_out_proj",
    )(a, x_all, w_o, ln_g, ln_b)


def _fox_layer(x_all, cache_k, cache_v, cache_lf, page_table, w):
    nt = T_PROMPT // FOX_TILE
    hm_shape = jax.ShapeDtypeStruct((BATCH, N_HEADS, T_PROMPT, HEAD_DIM), BF16)
    hm_spec = pl.BlockSpec((1, N_HEADS, FOX_TILE, HEAD_DIM), lambda b, t: (b, 0, t, 0))
    nat_shape = jax.ShapeDtypeStruct((BATCH, T_PROMPT, D), F32)
    nat_spec = pl.BlockSpec((1, FOX_TILE, D), lambda b, t: (b, t, 0))
    lf_shape = jax.ShapeDtypeStruct((BATCH, T_PROMPT, LANES), F32)
    lf_spec = pl.BlockSpec((1, FOX_TILE, LANES), lambda b, t: (b, t, 0))
    q_hm, k_hm, v_hm, k_p, v_p, lf_p, c_p = pl.pallas_call(
        _fox_proj_prompt_kernel,
        grid=(BATCH, nt),
        in_specs=[pl.BlockSpec((FOX_TILE, D), lambda b, t: (b * nt + t, 0)),
                  _full((D, FOX_NPAD)), _full((1, LANES))],
        out_specs=[hm_spec, hm_spec, hm_spec, nat_spec, nat_spec, lf_spec, lf_spec],
        out_shape=[jax.ShapeDtypeStruct(hm_shape.shape, F32), hm_shape, hm_shape, nat_shape, nat_shape,
                   lf_shape, lf_shape],
        scratch_shapes=[pltpu.VMEM((SUBLANES, LANES), F32)],
        compiler_params=_params(("arbitrary", "arbitrary")),
        name="fox_proj_prompt",
    )(x_all, w["w_in"], w["b_f"])

    ck_row = jnp.pad(jnp.transpose(c_p[:, :, :N_HEADS], (0, 2, 1)),
                     ((0, 0), (0, 0), (0, T_KPAD - T_PROMPT)))[:, :, None, :]
    nq = T_PROMPT // ATT_TQ
    kv_spec = pl.BlockSpec((1, ATT_HPB, T_PROMPT, HEAD_DIM), lambda b, h, q: (b, h, 0, 0))
    attn_p = pl.pallas_call(
        _attn_prompt_kernel,
        grid=(BATCH, N_HEADS // ATT_HPB, nq),
        in_specs=[pl.BlockSpec((1, ATT_HPB, ATT_TQ, HEAD_DIM), lambda b, h, q: (b, h, q, 0)),
                  kv_spec, kv_spec,
                  pl.BlockSpec((1, ATT_HPB, 1, T_KPAD), lambda b, h, q: (b, h, 0, 0)),
                  pl.BlockSpec((1, ATT_TQ, LANES), lambda b, h, q: (b, q, 0))],
        out_specs=pl.BlockSpec((1, ATT_TQ, ATT_HPB * HEAD_DIM), lambda b, h, q: (b, q, h)),
        out_shape=jax.ShapeDtypeStruct((BATCH, T_PROMPT, D), F32),
        compiler_params=_params(("arbitrary", "arbitrary", "arbitrary")),
        name="fox_attn_prompt",
    )(q_hm, k_hm, v_hm, ck_row, c_p)

    ns = N_SAMPLE // TOK_TILE
    tile = pl.BlockSpec((TOK_TILE, D), lambda i: (i, 0))
    q_s, k_s, v_s, lf_s = pl.pallas_call(
        _fox_proj_sample_kernel,
        grid=(ns,),
        in_specs=[pl.BlockSpec((TOK_TILE, D), lambda i: (SAMPLE_BLK0 + i, 0)),
                  _full((D, FOX_NPAD)), _full((1, LANES))],
        out_specs=[tile, tile, tile, pl.BlockSpec((TOK_TILE, LANES), lambda i: (i, 0))],
        out_shape=[jax.ShapeDtypeStruct((N_SAMPLE, D), F32), jax.ShapeDtypeStruct((N_SAMPLE, D), F32),
                   jax.ShapeDtypeStruct((N_SAMPLE, D), F32), jax.ShapeDtypeStruct((N_SAMPLE, LANES), F32)],
        compiler_params=_params(("arbitrary",)),
        name="fox_proj_sample",
    )(x_all, w["w_in"], w["b_f"])

    nrow = DEC_SEQ * N_HEADS
    ncol = PAGE_SIZE * N_HEADS
    by_head = lambda a: a.reshape(DEC_BATCH, nrow, HEAD_DIM)
    lfn = jnp.pad(lf_s[:, :N_HEADS].reshape(DEC_BATCH, 1, nrow), ((0, 0), (0, 0), (0, LANES - nrow)))
    seq_spec = pl.BlockSpec((1, nrow, HEAD_DIM), lambda s, p, pt: (s, 0, 0))
    page_specs = [pl.BlockSpec((1, ncol, HEAD_DIM), lambda s, p, pt, i=i: (pt[s, p * ATT_PPS + i], 0, 0))
                  for i in range(ATT_PPS)]
    lf_specs = [pl.BlockSpec((1, 1, ncol), lambda s, p, pt, i=i: (pt[s, p * ATT_PPS + i], 0, 0))
                for i in range(ATT_PPS)]
    attn_s = pl.pallas_call(
        _attn_sample_kernel,
        grid_spec=pltpu.PrefetchScalarGridSpec(
            num_scalar_prefetch=1,
            grid=(DEC_BATCH, N_PAGES // ATT_PPS),
            in_specs=[seq_spec] + page_specs + page_specs + lf_specs + [
                seq_spec, seq_spec, pl.BlockSpec((1, 1, LANES), lambda s, p, pt: (s, 0, 0))],
            out_specs=seq_spec,
            scratch_shapes=[pltpu.VMEM((nrow, HEAD_DIM), BF16), pltpu.VMEM((nrow, 1), F32),
                            pltpu.VMEM((nrow, 1), F32), pltpu.VMEM((nrow, HEAD_DIM), F32),
                            pltpu.VMEM((1, ncol), F32)]),
        out_shape=jax.ShapeDtypeStruct((DEC_BATCH, nrow, HEAD_DIM), F32),
        compiler_params=_params(("arbitrary", "arbitrary")),
        name="fox_attn_sample",
    )(page_table, by_head(q_s), *([cache_k] * ATT_PPS + [cache_v] * ATT_PPS + [cache_lf] * ATT_PPS),
      by_head(k_s), by_head(v_s), lfn)

    x_all = _out_proj(x_all, attn_p.reshape(N_PROMPT, D), w["w_o"], w["ln_g"], w["ln_b"], FOX_TILE, 0)
    x_all = _out_proj(x_all, attn_s.reshape(N_SAMPLE, D), w["w_o"], w["ln_g"], w["ln_b"], TOK_TILE,
                      SAMPLE_BLK0)
    return x_all, k_p, v_p, lf_p[:, :, :N_HEADS], k_s, v_s, lf_s[:, :N_HEADS]


def _row(v):
    return v.reshape(1, -1).astype(F32)


def kernel(x_prompt, x_sample, cache_k, cache_v, cache_logf, state_h, state_conv, page_table, meta_tokens,
           rg_w_in, rg_conv_w, rg_conv_b, rg_w_a, rg_b_a, rg_w_x, rg_b_x, rg_lambda, rg_w_out, fox_w_in,
           fox_b_f, fox_w_o, peer_w_q, peer_keys1, peer_keys2, peer_u, peer_v, ln_mix_g, ln_mix_b,
           ln_ffn_g, ln_ffn_b):
    meta = jnp.broadcast_to(meta_tokens.astype(F32)[None], (BATCH, N_META, D))
    hp = jnp.concatenate([meta, x_prompt], axis=1).reshape(N_PROMPT, D)
    x_all = jnp.concatenate([hp, jnp.zeros((SAMPLE_OFF - N_PROMPT, D), F32),
                             x_sample.reshape(N_SAMPLE, D)], axis=0)

    def peer_weights(i):
        return {"wq_t": peer_w_q[i].T.astype(BF16), "keys1": peer_keys1[i].astype(BF16),
                "keys2": peer_keys2[i].astype(BF16), "u": peer_u[i].astype(BF16),
                "v_t": peer_v[i].T.astype(BF16), "ln_g": _row(ln_ffn_g[i]), "ln_b": _row(ln_ffn_b[i])}

    rg = {"w_in": rg_w_in[0].astype(BF16), "conv_w": rg_conv_w[0].astype(F32), "conv_b": _row(rg_conv_b[0]),
          "w_ax": jnp.concatenate([rg_w_a[0], rg_w_x[0]], axis=-1).astype(BF16),
          "b_a": _row(rg_b_a[0]), "b_x": _row(rg_b_x[0]), "lam": _row(rg_lambda[0]),
          "w_out": rg_w_out[0].astype(BF16), "ln_g": _row(ln_mix_g[0]), "ln_b": _row(ln_mix_b[0])}
    x_all, ht_p, cbuf_p, ht_s, cbuf_s = _rg_layer(x_all, state_h[0], state_conv[0], rg)
    x_all = _peer_layer(x_all, peer_weights(0))

    n_pool = cache_k.shape[1]
    fox = {"w_in": jnp.pad(fox_w_in[0], ((0, 0), (0, FOX_NPAD - fox_w_in.shape[-1]))).astype(BF16),
           "b_f": jnp.pad(fox_b_f[0], (0, LANES - N_HEADS)).reshape(1, LANES).astype(F32),
           "w_o": fox_w_o[0].astype(BF16), "ln_g": _row(ln_mix_g[1]), "ln_b": _row(ln_mix_b[1])}
    x_all, k_p, v_p, lf_p, k_s, v_s, lf_s = _fox_layer(
        x_all, cache_k.reshape(n_pool, PAGE_SIZE * N_HEADS, HEAD_DIM),
        cache_v.reshape(n_pool, PAGE_SIZE * N_HEADS, HEAD_DIM),
        cache_logf.reshape(n_pool, 1, PAGE_SIZE * N_HEADS), page_table, fox)
    x_all = _peer_layer(x_all, peer_weights(1))

    y_prompt = x_all[:N_PROMPT].reshape(BATCH, T_PROMPT, D)[:, N_META:]
    y_sample = x_all[SAMPLE_OFF:].reshape(DEC_BATCH, DEC_SEQ, D)
    hd = (N_HEADS, HEAD_DIM)
    return (y_prompt, y_sample,
            ht_p[None], cbuf_p[None],
            k_p.reshape(1, BATCH, T_PROMPT, *hd), v_p.reshape(1, BATCH, T_PROMPT, *hd), lf_p[None],
            ht_s[None], cbuf_s[None],
            k_s.reshape(1, DEC_BATCH, DEC_SEQ, *hd), v_s.reshape(1, DEC_BATCH, DEC_SEQ, *hd),
            lf_s.reshape(1, DEC_BATCH, DEC_SEQ, N_HEADS))
```

```python
import functools

import jax
import jax.numpy as jnp
from jax import lax
from jax.experimental import pallas as pl
from jax.experimental.pallas import tpu as pltpu

F32 = jnp.float32
BF16 = jnp.bfloat16

D = 1024
BATCH = 8
N_META = 16
T_PROMPT = 2064
DEC_BATCH = 128
DEC_SEQ = 8
N_HEADS = 8
HEAD_DIM = 128
PAGE_SIZE = 128
N_PAGES = 16
RG_BLOCKS = 8
RG_BLOCK = 128
RG_C = 8.0
CONV_W = 4
PEER_HEADS = 8
N_KEYS = 128
N_EXPERTS = N_KEYS * N_KEYS
PEER_TOPK = 16
DEPTH = 2
ALPHA = (2.0 * DEPTH) ** 0.25
LN_EPS = 1e-5
ATTN_SCALE = HEAD_DIM ** -0.5

SUBLANES = 8
LANES = 128
VMEM_LIMIT = 56 * 1024 * 1024

N_PROMPT = BATCH * T_PROMPT
TOK_TILE = 512
SAMPLE_OFF = 16896
N_SAMPLE = DEC_BATCH * DEC_SEQ
N_TOT = SAMPLE_OFF + N_SAMPLE
SAMPLE_BLK0 = SAMPLE_OFF // TOK_TILE

RG_TILE = 344
FOX_TILE = 688
ATT_TQ = 344
ATT_TK = 256
ATT_HPB = 4
ATT_PPS = 8
T_KPAD = 2304
EXP_TILE = 1024
N_CAND = 17
FOX_NPAD = 3 * D + LANES

NT_DIMS = (((1,), (1,)), ((), ()))


GELU_C1 = 0.7978845608028654
GELU_C3 = GELU_C1 * 0.044715


def _gelu(x):
    return 0.5 * x * (1.0 + jnp.tanh(x * (GELU_C1 + GELU_C3 * (x * x))))


def _sigmoid(x):
    return 1.0 / (1.0 + jnp.exp(-x))


def _softplus(x):
    return jnp.maximum(x, 0.0) + jnp.log1p(jnp.exp(-jnp.abs(x)))


def _layer_norm(x, g, b):
    mu = jnp.mean(x, axis=-1, keepdims=True)
    xc = x - mu
    var = jnp.mean(xc * xc, axis=-1, keepdims=True)
    return xc * lax.rsqrt(var + LN_EPS) * g + b


def _full(shape):
    n = len(shape)
    return pl.BlockSpec(shape, lambda *_: (0,) * n)


def _params(sem, **kw):
    return pltpu.CompilerParams(dimension_semantics=sem, vmem_limit_bytes=VMEM_LIMIT, **kw)


def _rg_gates(rec, wax_ref, ba_ref, bx_ref, lam_ref):
    a_parts, i_parts = [], []
    for n in range(RG_BLOCKS):
        rn = rec[:, n * RG_BLOCK:(n + 1) * RG_BLOCK].astype(BF16)
        ax = jnp.dot(rn, wax_ref[n], preferred_element_type=F32)
        a_parts.append(ax[:, :RG_BLOCK])
        i_parts.append(ax[:, RG_BLOCK:])
    r = _sigmoid(jnp.concatenate(a_parts, axis=1) + ba_ref[...])
    i = _sigmoid(jnp.concatenate(i_parts, axis=1) + bx_ref[...])
    log_a = (-RG_C) * r * _softplus(-lam_ref[...])
    a = jnp.exp(log_a)
    bx = jnp.sqrt(1.0 - a * a) * (i * rec)
    return a, bx


def _group_scan(a, b):
    row = lax.broadcasted_iota(jnp.int32, a.shape, 0) & (SUBLANES - 1)
    for s in (1, 2, 4):
        a_sh = pltpu.roll(a, s, 0)
        b_sh = pltpu.roll(b, s, 0)
        m = row >= s
        b = jnp.where(m, a * b_sh + b, b)
        a = jnp.where(m, a * a_sh, a)
    return a, b


def _rg_prompt_kernel(x_ref, win_ref, cw_ref, cb_ref, wax_ref, ba_ref, bx_ref, lam_ref, wout_ref,
                      g_ref, b_ref, y_ref, ht_ref, cbuf_ref, hcar, tail, cs, a_s, b_s, h_s):
    t = pl.program_id(1)
    nt = pl.num_programs(1)
    rows = x_ref.shape[0]

    @pl.when(t == 0)
    def _():
        hcar[...] = jnp.zeros_like(hcar)
        tail[...] = jnp.zeros_like(tail)

    x = x_ref[...]
    xin = jnp.dot(x.astype(BF16), win_ref[...], preferred_element_type=F32)
    gate = _gelu(xin[:, :D])
    raw = xin[:, D:]
    cs[0:SUBLANES, :] = tail[...]
    cs[SUBLANES:SUBLANES + rows, :] = raw
    cw = cw_ref[...]
    rec = (cb_ref[...] + cw[3:4] * raw + cw[2:3] * cs[7:7 + rows, :]
           + cw[1:2] * cs[6:6 + rows, :] + cw[0:1] * cs[5:5 + rows, :])
    tail[...] = raw[rows - SUBLANES:rows, :]

    a, bx = _rg_gates(rec, wax_ref, ba_ref, bx_ref, lam_ref)
    a, bx = _group_scan(a, bx)
    a_s[...] = a
    b_s[...] = bx

    def body(g, h):
        r0 = pl.multiple_of(g * SUBLANES, SUBLANES)
        hg = a_s[pl.ds(r0, SUBLANES), :] * h + b_s[pl.ds(r0, SUBLANES), :]
        h_s[pl.ds(r0, SUBLANES), :] = hg
        return jnp.broadcast_to(hg[SUBLANES - 1:SUBLANES, :], (SUBLANES, D))

    hl = lax.fori_loop(0, rows // SUBLANES, body, hcar[...])
    hcar[...] = hl

    y = h_s[...] * gate
    mix = jnp.dot(y.astype(BF16), wout_ref[...], preferred_element_type=F32)
    y_ref[...] = _layer_norm(ALPHA * x + mix, g_ref[...], b_ref[...])

    @pl.when(t == nt - 1)
    def _():
        ht_ref[0] = hl[0:1, :]
        cbuf_ref[0] = cs[rows + 5:rows + 8, :]


def _rg_sample_kernel(x_ref, h0_ref, buf_ref, win_ref, cw_ref, cb_ref, wax_ref, ba_ref, bx_ref, lam_ref,
                      wout_ref, g_ref, b_ref, y_ref, hs_ref, raw_ref):
    rows = x_ref.shape[0]
    x = x_ref[...]
    xin = jnp.dot(x.astype(BF16), win_ref[...], preferred_element_type=F32)
    gate = _gelu(xin[:, :D])
    raw = xin[:, D:]
    tmod = lax.broadcasted_iota(jnp.int32, raw.shape, 0) & (SUBLANES - 1)
    bufv = buf_ref[...]
    cw = cw_ref[...]
    rec = cb_ref[...] + cw[3:4] * raw
    for j in (1, 2, 3):
        sh = pltpu.roll(raw, j, 0)
        bsh = pltpu.roll(bufv, rows + j - SUBLANES, 0)
        rec = rec + cw[3 - j:4 - j] * jnp.where(tmod >= j, sh, bsh)
    a, bx = _rg_gates(rec, wax_ref, ba_ref, bx_ref, lam_ref)
    a, bx = _group_scan(a, bx)
    hh = a * h0_ref[...] + bx
    mix = jnp.dot((hh * gate).astype(BF16), wout_ref[...], preferred_element_type=F32)
    y_ref[...] = _layer_norm(ALPHA * x + mix, g_ref[...], b_ref[...])
    hs_ref[...] = hh
    raw_ref[...] = raw


def _rg_layer(x_all, state_h, state_conv, w):
    weights = [w["w_in"], w["conv_w"], w["conv_b"], w["w_ax"], w["b_a"], w["b_x"], w["lam"], w["w_out"],
               w["ln_g"], w["ln_b"]]
    wspecs = [_full(a.shape) for a in weights]
    nt = T_PROMPT // RG_TILE
    x_all, ht_p, cbuf_p = pl.pallas_call(
        _rg_prompt_kernel,
        grid=(BATCH, nt),
        in_specs=[pl.BlockSpec((RG_TILE, D), lambda b, t: (b * nt + t, 0))] + wspecs,
        out_specs=[pl.BlockSpec((RG_TILE, D), lambda b, t: (b * nt + t, 0)),
                   pl.BlockSpec((1, 1, D), lambda b, t: (b, 0, 0)),
                   pl.BlockSpec((1, CONV_W - 1, D), lambda b, t: (b, 0, 0))],
        out_shape=[jax.ShapeDtypeStruct((N_TOT, D), F32),
                   jax.ShapeDtypeStruct((BATCH, 1, D), F32),
                   jax.ShapeDtypeStruct((BATCH, CONV_W - 1, D), F32)],
        scratch_shapes=[pltpu.VMEM((SUBLANES, D), F32), pltpu.VMEM((SUBLANES, D), F32),
                        pltpu.VMEM((RG_TILE + SUBLANES, D), F32), pltpu.VMEM((RG_TILE, D), F32),
                        pltpu.VMEM((RG_TILE, D), F32), pltpu.VMEM((RG_TILE, D), F32)],
        input_output_aliases={0: 0},
        compiler_params=_params(("arbitrary", "arbitrary")),
        name="rg_prompt",
    )(x_all, *weights)

    h0e = jnp.repeat(state_h, DEC_SEQ, axis=0)
    buf8 = jnp.pad(state_conv, ((0, 0), (DEC_SEQ - (CONV_W - 1), 0), (0, 0))).reshape(N_SAMPLE, D)
    tile = pl.BlockSpec((TOK_TILE, D), lambda i: (i, 0))
    xtile = pl.BlockSpec((TOK_TILE, D), lambda i: (SAMPLE_BLK0 + i, 0))
    x_all, hs, raws = pl.pallas_call(
        _rg_sample_kernel,
        grid=(N_SAMPLE // TOK_TILE,),
        in_specs=[xtile, tile, tile] + wspecs,
        out_specs=[xtile, tile, tile],
        out_shape=[jax.ShapeDtypeStruct((N_TOT, D), F32),
                   jax.ShapeDtypeStruct((N_SAMPLE, D), F32),
                   jax.ShapeDtypeStruct((N_SAMPLE, D), F32)],
        input_output_aliases={0: 0},
        compiler_params=_params(("arbitrary",)),
        name="rg_sample",
    )(x_all, h0e, buf8, *weights)
    ht_s = hs.reshape(DEC_BATCH, DEC_SEQ, D)[:, DEC_SEQ - 1]
    cbuf_s = raws.reshape(DEC_BATCH, DEC_SEQ, D)[:, DEC_SEQ - (CONV_W - 1):]
    return x_all, ht_p.reshape(BATCH, D), cbuf_p, ht_s, cbuf_s


def _extract_top(s, k, out_ref):
    for i in range(k):
        m = jnp.max(s, axis=0, keepdims=True)
        out_ref[i:i + 1, :] = m
        s = jnp.where(s == m, -jnp.inf, s)


def _extract_top_sorted(s, k, out_ref):
    n = s.shape[0] // SUBLANES
    assert n == 16 and k <= n + 1
    v = [s[SUBLANES * g:SUBLANES * (g + 1), :] for g in range(n)]
    size = 2
    while size <= n:
        j = size // 2
        while j >= 1:
            for a in range(n):
                b = a ^ j
                if b > a:
                    hi, lo = jnp.maximum(v[a], v[b]), jnp.minimum(v[a], v[b])
                    v[a], v[b] = (hi, lo) if (a & size) == 0 else (lo, hi)
            j //= 2
        size *= 2
    neg = jnp.full(v[0].shape, -jnp.inf, F32)
    for i in range(k):
        m = jnp.max(v[0], axis=0, keepdims=True)
        out_ref[i:i + 1, :] = m
        if i == k - 1:
            break
        win = v[0] == m
        for d in range(min(n, k - 1 - i)):
            v[d] = jnp.where(win, v[d + 1] if d + 1 < n else neg, v[d])


def _router_kernel(x_ref, wqt_ref, k1_ref, k2_ref, th_ref, cc_ref, s2_ref, e2_ref, v1s, v2s, cs):
    tt = x_ref.shape[0]
    xb = x_ref[...].astype(BF16)
    qt = lax.dot_general(wqt_ref[...], xb, NT_DIMS, preferred_element_type=F32)
    neg = jnp.full((3 * SUBLANES, tt), -jnp.inf, F32)
    rowi = lax.broadcasted_iota(jnp.int32, (SUBLANES, tt), 0)
    for h in range(PEER_HEADS):
        q1 = qt[h * 256:h * 256 + 128, :].astype(BF16)
        q2 = qt[h * 256 + 128:(h + 1) * 256, :].astype(BF16)
        s1 = jnp.dot(k1_ref[h], q1, preferred_element_type=F32)
        s2 = jnp.dot(k2_ref[h], q2, preferred_element_type=F32)
        v1s[...] = neg
        v2s[...] = neg
        _extract_top_sorted(s1, N_CAND, v1s)
        _extract_top_sorted(s2, N_CAND, v2s)
        v1 = v1s[...]
        v2 = v2s[...]
        blocks = [v1 + v2[0:1]]
        for b in range(1, SUBLANES):
            c = v1[0:SUBLANES] + v2[b:b + 1]
            lim = N_CAND // (b + 1)
            if lim < SUBLANES:
                c = jnp.where(rowi < lim, c, -jnp.inf)
            blocks.append(c)
        blocks.append(v2[SUBLANES:3 * SUBLANES] + v1[0:1])
        cs[...] = neg
        blocks.append(jnp.full((N_KEYS - 12 * SUBLANES, tt), -jnp.inf, F32))
        _extract_top_sorted(jnp.concatenate(blocks, axis=0), N_CAND, cs)
        top = cs[0:PEER_TOPK, :]
        z = jnp.sum(jnp.exp(top - top[0:1]), axis=0, keepdims=True)
        tau = 0.5 * (cs[PEER_TOPK - 1:PEER_TOPK, :] + cs[PEER_TOPK:PEER_TOPK + 1, :])
        th = tau - s1
        cc = jnp.exp(s1 - v1[0:1]) * (0.5 / z)
        e2 = jnp.exp(s2 - v2[0:1])
        for c in range(tt // LANES):
            ls = slice(c * LANES, (c + 1) * LANES)
            th_ref[h, c] = th[:, ls]
            cc_ref[h, c] = cc[:, ls]
            s2_ref[h, c] = s2[:, ls]
            e2_ref[h, c] = e2[:, ls]


def _expert_kernel(x_ref, u_ref, vt_ref, th_ref, cc_ref, s2_ref, e2_ref, g_ref, b_ref, y_ref,
                   xb_s, acc_s, act0, act1):
    s = pl.program_id(1)
    ns = pl.num_programs(1)
    tt = x_ref.shape[0]
    rows_per_step = u_ref.shape[0] // N_KEYS
    assert rows_per_step == SUBLANES
    nchunk = tt // LANES
    group = 2

    def project(act_ref):
        act_ref[:, 0:tt] = lax.dot_general(u_ref[...], xb_s[...], NT_DIMS, preferred_element_type=F32)

    def gate_and_mix(act_ref):
        a0 = pl.multiple_of((s - 1) * rows_per_step, SUBLANES)
        thg = [[th_ref[h, c, pl.ds(a0, SUBLANES), :] for h in range(PEER_HEADS)] for c in range(nchunk)]
        ccg = [[cc_ref[h, c, pl.ds(a0, SUBLANES), :] for h in range(PEER_HEADS)] for c in range(nchunk)]
        upd = None
        for r0 in range(0, rows_per_step, group):
            w_rows = []
            for r in range(r0, r0 + group):
                row = []
                for c in range(nchunk):
                    gsum = jnp.zeros((N_KEYS, LANES), F32)
                    for h in range(PEER_HEADS):
                        sel = jnp.where(s2_ref[h, c] >= thg[c][h][r:r + 1], e2_ref[h, c], 0.0)
                        gsum = gsum + sel * ccg[c][h][r:r + 1]
                    a_rc = act_ref[r * N_KEYS:(r + 1) * N_KEYS, c * LANES:(c + 1) * LANES]
                    ga = gsum * a_rc
                    th = jnp.tanh(a_rc * (GELU_C1 + GELU_C3 * (a_rc * a_rc)))
                    row.append((ga + ga * th).astype(BF16))
                w_rows.append(jnp.concatenate(row, axis=1))
            w = jnp.concatenate(w_rows, axis=0)
            part = jnp.dot(vt_ref[:, r0 * N_KEYS:(r0 + group) * N_KEYS], w, preferred_element_type=F32)
            upd = part if upd is None else upd + part
        acc_s[...] += upd

    @pl.when(s == 0)
    def _():
        xb_s[...] = x_ref[...].astype(BF16)
        acc_s[...] = jnp.zeros_like(acc_s)
        project(act0)

    middle = jnp.logical_and(s > 0, s < ns - 1)

    @pl.when(jnp.logical_and(middle, (s & 1) == 1))
    def _():
        gate_and_mix(act0)
        project(act1)

    @pl.when(jnp.logical_and(middle, (s & 1) == 0))
    def _():
        gate_and_mix(act1)
        project(act0)

    @pl.when(s == ns - 1)
    def _():
        gate_and_mix(act1)
        y_ref[...] = _layer_norm(ALPHA * x_ref[...] + acc_s[...].T, g_ref[...], b_ref[...])


def _peer_layer(x_all, w):
    nt = N_TOT // TOK_TILE
    cpt = TOK_TILE // LANES
    gate_shape = jax.ShapeDtypeStruct((PEER_HEADS, N_TOT // LANES, N_KEYS, LANES), F32)
    gate_spec = pl.BlockSpec((PEER_HEADS, cpt, N_KEYS, LANES), lambda i: (0, i, 0, 0))
    th, cc, s2, e2 = pl.pallas_call(
        _router_kernel,
        grid=(nt,),
        in_specs=[pl.BlockSpec((TOK_TILE, D), lambda i: (i, 0)),
                  _full(w["wq_t"].shape), _full(w["keys1"].shape), _full(w["keys2"].shape)],
        out_specs=[gate_spec] * 4,
        out_shape=[gate_shape] * 4,
        scratch_shapes=[pltpu.VMEM((3 * SUBLANES, TOK_TILE), F32)] * 3,
        compiler_params=_params(("arbitrary",)),
        name="peer_router",
    )(x_all, w["wq_t"], w["keys1"], w["keys2"])

    ne = N_EXPERTS // EXP_TILE
    assert ne % 2 == 0
    gate_spec2 = pl.BlockSpec((PEER_HEADS, cpt, N_KEYS, LANES), lambda i, s: (0, i, 0, 0))
    pitch = TOK_TILE + LANES
    return pl.pallas_call(
        _expert_kernel,
        grid=(nt, ne + 1),
        in_specs=[pl.BlockSpec((TOK_TILE, D), lambda i, s: (i, 0)),
                  pl.BlockSpec((EXP_TILE, D), lambda i, s: (jnp.minimum(s, ne - 1), 0)),
                  pl.BlockSpec((D, EXP_TILE), lambda i, s: (0, jnp.maximum(s - 1, 0))),
                  gate_spec2, gate_spec2, gate_spec2, gate_spec2,
                  _full((1, D)), _full((1, D))],
        out_specs=pl.BlockSpec((TOK_TILE, D), lambda i, s: (i, 0)),
        out_shape=jax.ShapeDtypeStruct((N_TOT, D), F32),
        scratch_shapes=[pltpu.VMEM((TOK_TILE, D), BF16), pltpu.VMEM((D, TOK_TILE), F32),
                        pltpu.VMEM((EXP_TILE, pitch), F32), pltpu.VMEM((EXP_TILE, pitch), F32)],
        compiler_params=_params(("arbitrary", "arbitrary")),
        name="peer_experts",
    )(x_all, w["u"], w["v_t"], th, cc, s2, e2, w["ln_g"], w["ln_b"])


def _cumsum(x, axis, stride=1):
    n = x.shape[axis]
    idx = lax.broadcasted_iota(jnp.int32, x.shape, axis)
    s = stride
    while s < n:
        x = x + jnp.where(idx >= s, pltpu.roll(x, s, axis), 0.0)
        s *= 2
    return x


def _log_forget(zf, bf):
    zf = zf + bf
    lf = jnp.minimum(zf, 0.0) - jnp.log1p(jnp.exp(-jnp.abs(zf)))
    lane = lax.broadcasted_iota(jnp.int32, lf.shape, 1)
    return jnp.where(lane < N_HEADS, lf, 0.0)


def _fox_proj_prompt_kernel(x_ref, w_ref, bf_ref, q_ref, khm_ref, vhm_ref, k_ref, v_ref, lf_ref, c_ref,
                            ccar):
    t = pl.program_id(1)
    rows = x_ref.shape[0]

    @pl.when(t == 0)
    def _():
        ccar[...] = jnp.zeros_like(ccar)

    z = jnp.dot(x_ref[...].astype(BF16), w_ref[...], preferred_element_type=F32)
    for h in range(N_HEADS):
        q_ref[0, h] = z[:, h * HEAD_DIM:(h + 1) * HEAD_DIM]
        khm_ref[0, h] = z[:, D + h * HEAD_DIM:D + (h + 1) * HEAD_DIM].astype(BF16)
        vhm_ref[0, h] = z[:, 2 * D + h * HEAD_DIM:2 * D + (h + 1) * HEAD_DIM].astype(BF16)
    k_ref[0] = z[:, D:2 * D]
    v_ref[0] = z[:, 2 * D:3 * D]
    lf = _log_forget(z[:, 3 * D:], bf_ref[...])
    lf_ref[0] = lf
    c = _cumsum(lf, 0) + ccar[0:1, :]
    c_ref[0] = c
    ccar[...] = jnp.broadcast_to(c[rows - 1:rows, :], ccar.shape)


def _fox_proj_sample_kernel(x_ref, w_ref, bf_ref, q_ref, k_ref, v_ref, lf_ref):
    z = jnp.dot(x_ref[...].astype(BF16), w_ref[...], preferred_element_type=F32)
    q_ref[...] = z[:, :D]
    k_ref[...] = z[:, D:2 * D]
    v_ref[...] = z[:, 2 * D:3 * D]
    lf_ref[...] = _log_forget(z[:, 3 * D:], bf_ref[...])


def _attn_prompt_kernel(q_ref, k_ref, v_ref, ck_ref, c_ref, o_ref):
    hg = pl.program_id(1)
    qt = pl.program_id(2)
    lane = lax.broadcasted_iota(jnp.int32, (ATT_TQ, LANES), 1)
    c_blk = c_ref[0]
    qs = [q_ref[0, i].astype(BF16) for i in range(ATT_HPB)]
    cqs = [jnp.sum(jnp.where(lane == hg * ATT_HPB + i, c_blk, 0.0), axis=1, keepdims=True)
           for i in range(ATT_HPB)]
    qpos = qt * ATT_TQ + lax.broadcasted_iota(jnp.int32, (ATT_TQ, 1), 0)

    def chunk(rows, tk, kpos0, carry):
        kpos = kpos0 + lax.broadcasted_iota(jnp.int32, (ATT_TQ, tk), 1)
        visible = kpos <= qpos
        out = []
        for i in range(ATT_HPB):
            m, l, acc = carry[3 * i:3 * i + 3]
            s = lax.dot_general(qs[i], k_ref[0, i, rows, :], NT_DIMS, preferred_element_type=F32)
            s = jnp.where(visible, s * ATTN_SCALE + cqs[i] - ck_ref[0, i, :, rows], -jnp.inf)
            mn = jnp.maximum(m, jnp.max(s, axis=1, keepdims=True))
            alpha = jnp.exp(m - mn)
            p = jnp.exp(s - mn)
            l = alpha * l + jnp.sum(p, axis=1, keepdims=True)
            acc = alpha * acc + jnp.dot(p.astype(BF16), v_ref[0, i, rows, :], preferred_element_type=F32)
            out += [mn, l, acc]
        return tuple(out)

    def body(c, carry):
        k0 = pl.multiple_of(c * ATT_TK, ATT_TK)
        return chunk(pl.ds(k0, ATT_TK), ATT_TK, k0, carry)

    n_main = (T_PROMPT // ATT_TK) * ATT_TK
    last_q = qt * ATT_TQ + (ATT_TQ - 1)
    n_chunks = jnp.minimum(lax.shift_right_logical(last_q, 8) + 1, n_main // ATT_TK)
    carry = (jnp.full((ATT_TQ, 1), -jnp.inf, F32), jnp.zeros((ATT_TQ, 1), F32),
             jnp.zeros((ATT_TQ, HEAD_DIM), F32)) * ATT_HPB
    carry = lax.fori_loop(0, n_chunks, body, carry)
    carry = chunk(slice(n_main, T_PROMPT), T_PROMPT - n_main, n_main, carry)
    for i in range(ATT_HPB):
        o_ref[0, :, i * HEAD_DIM:(i + 1) * HEAD_DIM] = carry[3 * i + 2] * (1.0 / carry[3 * i + 1])


def _attn_sample_kernel(pt_ref, q_ref, *refs):
    del pt_ref
    kc_refs = refs[0:ATT_PPS]
    vc_refs = refs[ATT_PPS:2 * ATT_PPS]
    lf_refs = refs[2 * ATT_PPS:3 * ATT_PPS]
    kn_ref, vn_ref, lfn_ref, o_ref, qb, m_s, l_s, acc, car = refs[3 * ATT_PPS:]
    p = pl.program_id(1)
    np_ = pl.num_programs(1)
    nrow = DEC_SEQ * N_HEADS
    ncol = PAGE_SIZE * N_HEADS

    @pl.when(p == 0)
    def _():
        qb[...] = q_ref[0].astype(BF16)
        m_s[...] = jnp.full(m_s.shape, -jnp.inf, F32)
        l_s[...] = jnp.zeros_like(l_s)
        acc[...] = jnp.zeros_like(acc)
        car[...] = jnp.zeros_like(car)

    def update(kbs, vbs, biases, mask):
        q = qb[...]
        ss = [jnp.where(mask, lax.dot_general(q, kb, NT_DIMS, preferred_element_type=F32) * ATTN_SCALE - bias,
                        -jnp.inf) for kb, bias in zip(kbs, biases)]
        m_old = m_s[...]
        mn = m_old
        for sv in ss:
            mn = jnp.maximum(mn, jnp.max(sv, axis=1, keepdims=True))
        alpha = jnp.exp(m_old - mn)
        lsum = alpha * l_s[...]
        av = alpha * acc[...]
        for sv, vb in zip(ss, vbs):
            pr = jnp.exp(sv - mn)
            lsum = lsum + jnp.sum(pr, axis=1, keepdims=True)
            av = av + jnp.dot(pr.astype(BF16), vb, preferred_element_type=F32)
        l_s[...] = lsum
        acc[...] = av
        m_s[...] = mn

    def same_head(n):
        r = lax.broadcasted_iota(jnp.int32, (nrow, n), 0)
        c = lax.broadcasted_iota(jnp.int32, (nrow, n), 1)
        return (r & (N_HEADS - 1)) == (c & (N_HEADS - 1)), r, c

    run = car[...]
    biases = []
    for lf_ref in lf_refs:
        lf = lf_ref[0]
        biases.append(_cumsum(lf, 1, N_HEADS) + run)
        tot = lf
        sft = N_HEADS
        while sft < ncol:
            tot = tot + pltpu.roll(tot, sft, 1)
            sft *= 2
        run = run + tot
    car[...] = run
    mask, _, _ = same_head(ncol)
    update([r[0].astype(BF16) for r in kc_refs], [r[0].astype(BF16) for r in vc_refs], biases, mask)

    @pl.when(p == np_ - 1)
    def _():
        zpad = jnp.zeros((LANES - nrow, HEAD_DIM), F32)
        kb = jnp.concatenate([kn_ref[0], zpad], axis=0).astype(BF16)
        vb = jnp.concatenate([vn_ref[0], zpad], axis=0).astype(BF16)
        bias = _cumsum(lfn_ref[0], 1, N_HEADS) + car[:, 0:LANES]
        head_ok, r, c = same_head(LANES)
        causal = lax.shift_right_logical(c, 3) <= lax.shift_right_logical(r, 3)
        update([kb], [vb], [bias], head_ok & causal & (c < nrow))
        o_ref[0] = acc[...] * (1.0 / l_s[...])


def _out_proj_kernel(a_ref, x_ref, w_ref, g_ref, b_ref, y_ref):
    mix = jnp.dot(a_ref[...].astype(BF16), w_ref[...], preferred_element_type=F32)
    y_ref[...] = _layer_norm(ALPHA * x_ref[...] + mix, g_ref[...], b_ref[...])


def _out_proj(x_all, a, w_o, ln_g, ln_b, tile, blk0):
    n = a.shape[0] // tile
    xspec = pl.BlockSpec((tile, D), lambda i: (blk0 + i, 0))
    return pl.pallas_call(
        _out_proj_kernel,
        grid=(n,),
        in_specs=[pl.BlockSpec((tile, D), lambda i: (i, 0)), xspec, _full((D, D)), _full((1, D)),
                  _full((1, D))],
        out_specs=xspec,
        out_shape=jax.ShapeDtypeStruct((N_TOT, D), F32),
        input_output_aliases={1: 0},
        compiler_params=_params(("arbitrary",)),
        name="fox_out_proj",
    )(a, x_all, w_o, ln_g, ln_b)


def _fox_layer(x_all, cache_k, cache_v, cache_lf, page_table, w):
    nt = T_PROMPT // FOX_TILE
    hm_shape = jax.ShapeDtypeStruct((BATCH, N_HEADS, T_PROMPT, HEAD_DIM), BF16)
    hm_spec = pl.BlockSpec((1, N_HEADS, FOX_TILE, HEAD_DIM), lambda b, t: (b, 0, t, 0))
    nat_shape = jax.ShapeDtypeStruct((BATCH, T_PROMPT, D), F32)
    nat_spec = pl.BlockSpec((1, FOX_TILE, D), lambda b, t: (b, t, 0))
    lf_shape = jax.ShapeDtypeStruct((BATCH, T_PROMPT, LANES), F32)
    lf_spec = pl.BlockSpec((1, FOX_TILE, LANES), lambda b, t: (b, t, 0))
    q_hm, k_hm, v_hm, k_p, v_p, lf_p, c_p = pl.pallas_call(
        _fox_proj_prompt_kernel,
        grid=(BATCH, nt),
        in_specs=[pl.BlockSpec((FOX_TILE, D), lambda b, t: (b * nt + t, 0)),
                  _full((D, FOX_NPAD)), _full((1, LANES))],
        out_specs=[hm_spec, hm_spec, hm_spec, nat_spec, nat_spec, lf_spec, lf_spec],
        out_shape=[jax.ShapeDtypeStruct(hm_shape.shape, F32), hm_shape, hm_shape, nat_shape, nat_shape,
                   lf_shape, lf_shape],
        scratch_shapes=[pltpu.VMEM((SUBLANES, LANES), F32)],
        compiler_params=_params(("arbitrary", "arbitrary")),
        name="fox_proj_prompt",
    )(x_all, w["w_in"], w["b_f"])

    ck_row = jnp.pad(jnp.transpose(c_p[:, :, :N_HEADS], (0, 2, 1)),
                     ((0, 0), (0, 0), (0, T_KPAD - T_PROMPT)))[:, :, None, :]
    nq = T_PROMPT // ATT_TQ
    kv_spec = pl.BlockSpec((1, ATT_HPB, T_PROMPT, HEAD_DIM), lambda b, h, q: (b, h, 0, 0))
    attn_p = pl.pallas_call(
        _attn_prompt_kernel,
        grid=(BATCH, N_HEADS // ATT_HPB, nq),
        in_specs=[pl.BlockSpec((1, ATT_HPB, ATT_TQ, HEAD_DIM), lambda b, h, q: (b, h, q, 0)),
                  kv_spec, kv_spec,
                  pl.BlockSpec((1, ATT_HPB, 1, T_KPAD), lambda b, h, q: (b, h, 0, 0)),
                  pl.BlockSpec((1, ATT_TQ, LANES), lambda b, h, q: (b, q, 0))],
        out_specs=pl.BlockSpec((1, ATT_TQ, ATT_HPB * HEAD_DIM), lambda b, h, q: (b, q, h)),
        out_shape=jax.ShapeDtypeStruct((BATCH, T_PROMPT, D), F32),
        compiler_params=_params(("arbitrary", "arbitrary", "arbitrary")),
        name="fox_attn_prompt",
    )(q_hm, k_hm, v_hm, ck_row, c_p)

    ns = N_SAMPLE // TOK_TILE
    tile = pl.BlockSpec((TOK_TILE, D), lambda i: (i, 0))
    q_s, k_s, v_s, lf_s = pl.pallas_call(
        _fox_proj_sample_kernel,
        grid=(ns,),
        in_specs=[pl.BlockSpec((TOK_TILE, D), lambda i: (SAMPLE_BLK0 + i, 0)),
                  _full((D, FOX_NPAD)), _full((1, LANES))],
        out_specs=[tile, tile, tile, pl.BlockSpec((TOK_TILE, LANES), lambda i: (i, 0))],
        out_shape=[jax.ShapeDtypeStruct((N_SAMPLE, D), F32), jax.ShapeDtypeStruct((N_SAMPLE, D), F32),
                   jax.ShapeDtypeStruct((N_SAMPLE, D), F32), jax.ShapeDtypeStruct((N_SAMPLE, LANES), F32)],
        compiler_params=_params(("arbitrary",)),
        name="fox_proj_sample",
    )(x_all, w["w_in"], w["b_f"])

    nrow = DEC_SEQ * N_HEADS
    ncol = PAGE_SIZE * N_HEADS
    by_head = lambda a: a.reshape(DEC_BATCH, nrow, HEAD_DIM)
    lfn = jnp.pad(lf_s[:, :N_HEADS].reshape(DEC_BATCH, 1, nrow), ((0, 0), (0, 0), (0, LANES - nrow)))
    seq_spec = pl.BlockSpec((1, nrow, HEAD_DIM), lambda s, p, pt: (s, 0, 0))
    page_specs = [pl.BlockSpec((1, ncol, HEAD_DIM), lambda s, p, pt, i=i: (pt[s, p * ATT_PPS + i], 0, 0))
                  for i in range(ATT_PPS)]
    lf_specs = [pl.BlockSpec((1, 1, ncol), lambda s, p, pt, i=i: (pt[s, p * ATT_PPS + i], 0, 0))
                for i in range(ATT_PPS)]
    attn_s = pl.pallas_call(
        _attn_sample_kernel,
        grid_spec=pltpu.PrefetchScalarGridSpec(
            num_scalar_prefetch=1,
            grid=(DEC_BATCH, N_PAGES // ATT_PPS),
            in_specs=[seq_spec] + page_specs + page_specs + lf_specs + [
                seq_spec, seq_spec, pl.BlockSpec((1, 1, LANES), lambda s, p, pt: (s, 0, 0))],
            out_specs=seq_spec,
            scratch_shapes=[pltpu.VMEM((nrow, HEAD_DIM), BF16), pltpu.VMEM((nrow, 1), F32),
                            pltpu.VMEM((nrow, 1), F32), pltpu.VMEM((nrow, HEAD_DIM), F32),
                            pltpu.VMEM((1, ncol), F32)]),
        out_shape=jax.ShapeDtypeStruct((DEC_BATCH, nrow, HEAD_DIM), F32),
        compiler_params=_params(("arbitrary", "arbitrary")),
        name="fox_attn_sample",
    )(page_table, by_head(q_s), *([cache_k] * ATT_PPS + [cache_v] * ATT_PPS + [cache_lf] * ATT_PPS),
      by_head(k_s), by_head(v_s), lfn)

    x_all = _out_proj(x_all, attn_p.reshape(N_PROMPT, D), w["w_o"], w["ln_g"], w["ln_b"], FOX_TILE, 0)
    x_all = _out_proj(x_all, attn_s.reshape(N_SAMPLE, D), w["w_o"], w["ln_g"], w["ln_b"], TOK_TILE,
                      SAMPLE_BLK0)
    return x_all, k_p, v_p, lf_p[:, :, :N_HEADS], k_s, v_s, lf_s[:, :N_HEADS]


def _row(v):
    return v.reshape(1, -1).astype(F32)


def kernel(x_prompt, x_sample, cache_k, cache_v, cache_logf, state_h, state_conv, page_table, meta_tokens,
           rg_w_in, rg_conv_w, rg_conv_b, rg_w_a, rg_b_a, rg_w_x, rg_b_x, rg_lambda, rg_w_out, fox_w_in,
           fox_b_f, fox_w_o, peer_w_q, peer_keys1, peer_keys2, peer_u, peer_v, ln_mix_g, ln_mix_b,
           ln_ffn_g, ln_ffn_b):
    meta = jnp.broadcast_to(meta_tokens.astype(F32)[None], (BATCH, N_META, D))
    hp = jnp.concatenate([meta, x_prompt], axis=1).reshape(N_PROMPT, D)
    x_all = jnp.concatenate([hp, jnp.zeros((SAMPLE_OFF - N_PROMPT, D), F32),
                             x_sample.reshape(N_SAMPLE, D)], axis=0)

    def peer_weights(i):
        return {"wq_t": peer_w_q[i].T.astype(BF16), "keys1": peer_keys1[i].astype(BF16),
                "keys2": peer_keys2[i].astype(BF16), "u": peer_u[i].astype(BF16),
                "v_t": peer_v[i].T.astype(BF16), "ln_g": _row(ln_ffn_g[i]), "ln_b": _row(ln_ffn_b[i])}

    rg = {"w_in": rg_w_in[0].astype(BF16), "conv_w": rg_conv_w[0].astype(F32), "conv_b": _row(rg_conv_b[0]),
          "w_ax": jnp.concatenate([rg_w_a[0], rg_w_x[0]], axis=-1).astype(BF16),
          "b_a": _row(rg_b_a[0]), "b_x": _row(rg_b_x[0]), "lam": _row(rg_lambda[0]),
          "w_out": rg_w_out[0].astype(BF16), "ln_g": _row(ln_mix_g[0]), "ln_b": _row(ln_mix_b[0])}
    x_all, ht_p, cbuf_p, ht_s, cbuf_s = _rg_layer(x_all, state_h[0], state_conv[0], rg)
    x_all = _peer_layer(x_all, peer_weights(0))

    n_pool = cache_k.shape[1]
    fox = {"w_in": jnp.pad(fox_w_in[0], ((0, 0), (0, FOX_NPAD - fox_w_in.shape[-1]))).astype(BF16),
           "b_f": jnp.pad(fox_b_f[0], (0, LANES - N_HEADS)).reshape(1, LANES).astype(F32),
           "w_o": fox_w_o[0].astype(BF16), "ln_g": _row(ln_mix_g[1]), "ln_b": _row(ln_mix_b[1])}
    x_all, k_p, v_p, lf_p, k_s, v_s, lf_s = _fox_layer(
        x_all, cache_k.reshape(n_pool, PAGE_SIZE * N_HEADS, HEAD_DIM),
        cache_v.reshape(n_pool, PAGE_SIZE * N_HEADS, HEAD_DIM),
        cache_logf.reshape(n_pool, 1, PAGE_SIZE * N_HEADS), page_table, fox)
    x_all = _peer_layer(x_all, peer_weights(1))

    y_prompt = x_all[:N_PROMPT].reshape(BATCH, T_PROMPT, D)[:, N_META:]
    y_sample = x_all[SAMPLE_OFF:].reshape(DEC_BATCH, DEC_SEQ, D)
    hd = (N_HEADS, HEAD_DIM)
    return (y_prompt, y_sample,
            ht_p[None], cbuf_p[None],
            k_p.reshape(1, BATCH, T_PROMPT, *hd), v_p.reshape(1, BATCH, T_PROMPT, *hd), lf_p[None],
            ht_s[None], cbuf_s[None],
            k_s.reshape(1, DEC_BATCH, DEC_SEQ, *hd), v_s.reshape(1, DEC_BATCH, DEC_SEQ, *hd),
            lf_s.reshape(1, DEC_BATCH, DEC_SEQ, N_HEADS))
```

```python
import functools

import jax
import jax.numpy as jnp
from jax import lax
from jax.experimental import pallas as pl
from jax.experimental.pallas import tpu as pltpu

F32 = jnp.float32
BF16 = jnp.bfloat16

D = 1024
BATCH = 8
N_META = 16
T_PROMPT = 2064
DEC_BATCH = 128
DEC_SEQ = 8
N_HEADS = 8
HEAD_DIM = 128
PAGE_SIZE = 128
N_PAGES = 16
RG_BLOCKS = 8
RG_BLOCK = 128
RG_C = 8.0
CONV_W = 4
PEER_HEADS = 8
N_KEYS = 128
N_EXPERTS = N_KEYS * N_KEYS
PEER_TOPK = 16
DEPTH = 2
ALPHA = (2.0 * DEPTH) ** 0.25
LN_EPS = 1e-5
ATTN_SCALE = HEAD_DIM ** -0.5

SUBLANES = 8
LANES = 128
VMEM_LIMIT = 56 * 1024 * 1024

N_PROMPT = BATCH * T_PROMPT
TOK_TILE = 512
SAMPLE_OFF = 16896
N_SAMPLE = DEC_BATCH * DEC_SEQ
N_TOT = SAMPLE_OFF + N_SAMPLE
SAMPLE_BLK0 = SAMPLE_OFF // TOK_TILE

RG_TILE = 344
FOX_TILE = 688
ATT_TQ = 344
ATT_TK = 512
ATT_HPB = 4
ATT_PPS = 8
T_KPAD = 2304
EXP_TILE = 1024
N_CAND = 17
FOX_NPAD = 3 * D + LANES

NT_DIMS = (((1,), (1,)), ((), ()))


GELU_C1 = 0.7978845608028654
GELU_C3 = GELU_C1 * 0.044715


def _gelu(x):
    return 0.5 * x * (1.0 + jnp.tanh(x * (GELU_C1 + GELU_C3 * (x * x))))


def _sigmoid(x):
    return 1.0 / (1.0 + jnp.exp(-x))


def _softplus(x):
    return jnp.maximum(x, 0.0) + jnp.log1p(jnp.exp(-jnp.abs(x)))


def _layer_norm(x, g, b):
    mu = jnp.mean(x, axis=-1, keepdims=True)
    xc = x - mu
    var = jnp.mean(xc * xc, axis=-1, keepdims=True)
    return xc * lax.rsqrt(var + LN_EPS) * g + b


def _full(shape):
    n = len(shape)
    return pl.BlockSpec(shape, lambda *_: (0,) * n)


def _params(sem, **kw):
    return pltpu.CompilerParams(dimension_semantics=sem, vmem_limit_bytes=VMEM_LIMIT, **kw)


def _rg_gates(rec, wax_ref, ba_ref, bx_ref, lam_ref):
    a_parts, i_parts = [], []
    for n in range(RG_BLOCKS):
        rn = rec[:, n * RG_BLOCK:(n + 1) * RG_BLOCK].astype(BF16)
        ax = jnp.dot(rn, wax_ref[n], preferred_element_type=F32)
        a_parts.append(ax[:, :RG_BLOCK])
        i_parts.append(ax[:, RG_BLOCK:])
    r = _sigmoid(jnp.concatenate(a_parts, axis=1) + ba_ref[...])
    i = _sigmoid(jnp.concatenate(i_parts, axis=1) + bx_ref[...])
    log_a = (-RG_C) * r * _softplus(-lam_ref[...])
    a = jnp.exp(log_a)
    bx = jnp.sqrt(1.0 - a * a) * (i * rec)
    return a, bx


def _group_scan(a, b):
    row = lax.broadcasted_iota(jnp.int32, a.shape, 0) & (SUBLANES - 1)
    for s in (1, 2, 4):
        a_sh = pltpu.roll(a, s, 0)
        b_sh = pltpu.roll(b, s, 0)
        m = row >= s
        b = jnp.where(m, a * b_sh + b, b)
        a = jnp.where(m, a * a_sh, a)
    return a, b


def _rg_prompt_kernel(x_ref, win_ref, cw_ref, cb_ref, wax_ref, ba_ref, bx_ref, lam_ref, wout_ref,
                      g_ref, b_ref, y_ref, ht_ref, cbuf_ref, hcar, tail, cs, a_s, b_s, h_s):
    t = pl.program_id(1)
    nt = pl.num_programs(1)
    rows = x_ref.shape[0]

    @pl.when(t == 0)
    def _():
        hcar[...] = jnp.zeros_like(hcar)
        tail[...] = jnp.zeros_like(tail)

    x = x_ref[...]
    xin = jnp.dot(x.astype(BF16), win_ref[...], preferred_element_type=F32)
    gate = _gelu(xin[:, :D])
    raw = xin[:, D:]
    cs[0:SUBLANES, :] = tail[...]
    cs[SUBLANES:SUBLANES + rows, :] = raw
    cw = cw_ref[...]
    rec = (cb_ref[...] + cw[3:4] * raw + cw[2:3] * cs[7:7 + rows, :]
           + cw[1:2] * cs[6:6 + rows, :] + cw[0:1] * cs[5:5 + rows, :])
    tail[...] = raw[rows - SUBLANES:rows, :]

    a, bx = _rg_gates(rec, wax_ref, ba_ref, bx_ref, lam_ref)
    a, bx = _group_scan(a, bx)
    a_s[...] = a
    b_s[...] = bx

    def body(g, h):
        r0 = pl.multiple_of(g * SUBLANES, SUBLANES)
        hg = a_s[pl.ds(r0, SUBLANES), :] * h + b_s[pl.ds(r0, SUBLANES), :]
        h_s[pl.ds(r0, SUBLANES), :] = hg
        return jnp.broadcast_to(hg[SUBLANES - 1:SUBLANES, :], (SUBLANES, D))

    hl = lax.fori_loop(0, rows // SUBLANES, body, hcar[...])
    hcar[...] = hl

    y = h_s[...] * gate
    mix = jnp.dot(y.astype(BF16), wout_ref[...], preferred_element_type=F32)
    y_ref[...] = _layer_norm(ALPHA * x + mix, g_ref[...], b_ref[...])

    @pl.when(t == nt - 1)
    def _():
        ht_ref[0] = hl[0:1, :]
        cbuf_ref[0] = cs[rows + 5:rows + 8, :]


def _rg_sample_kernel(x_ref, h0_ref, buf_ref, win_ref, cw_ref, cb_ref, wax_ref, ba_ref, bx_ref, lam_ref,
                      wout_ref, g_ref, b_ref, y_ref, hs_ref, raw_ref):
    rows = x_ref.shape[0]
    x = x_ref[...]
    xin = jnp.dot(x.astype(BF16), win_ref[...], preferred_element_type=F32)
    gate = _gelu(xin[:, :D])
    raw = xin[:, D:]
    tmod = lax.broadcasted_iota(jnp.int32, raw.shape, 0) & (SUBLANES - 1)
    bufv = buf_ref[...]
    cw = cw_ref[...]
    rec = cb_ref[...] + cw[3:4] * raw
    for j in (1, 2, 3):
        sh = pltpu.roll(raw, j, 0)
        bsh = pltpu.roll(bufv, rows + j - SUBLANES, 0)
        rec = rec + cw[3 - j:4 - j] * jnp.where(tmod >= j, sh, bsh)
    a, bx = _rg_gates(rec, wax_ref, ba_ref, bx_ref, lam_ref)
    a, bx = _group_scan(a, bx)
    hh = a * h0_ref[...] + bx
    mix = jnp.dot((hh * gate).astype(BF16), wout_ref[...], preferred_element_type=F32)
    y_ref[...] = _layer_norm(ALPHA * x + mix, g_ref[...], b_ref[...])
    hs_ref[...] = hh
    raw_ref[...] = raw


def _rg_layer(x_all, state_h, state_conv, w):
    weights = [w["w_in"], w["conv_w"], w["conv_b"], w["w_ax"], w["b_a"], w["b_x"], w["lam"], w["w_out"],
               w["ln_g"], w["ln_b"]]
    wspecs = [_full(a.shape) for a in weights]
    nt = T_PROMPT // RG_TILE
    x_all, ht_p, cbuf_p = pl.pallas_call(
        _rg_prompt_kernel,
        grid=(BATCH, nt),
        in_specs=[pl.BlockSpec((RG_TILE, D), lambda b, t: (b * nt + t, 0))] + wspecs,
        out_specs=[pl.BlockSpec((RG_TILE, D), lambda b, t: (b * nt + t, 0)),
                   pl.BlockSpec((1, 1, D), lambda b, t: (b, 0, 0)),
                   pl.BlockSpec((1, CONV_W - 1, D), lambda b, t: (b, 0, 0))],
        out_shape=[jax.ShapeDtypeStruct((N_TOT, D), F32),
                   jax.ShapeDtypeStruct((BATCH, 1, D), F32),
                   jax.ShapeDtypeStruct((BATCH, CONV_W - 1, D), F32)],
        scratch_shapes=[pltpu.VMEM((SUBLANES, D), F32), pltpu.VMEM((SUBLANES, D), F32),
                        pltpu.VMEM((RG_TILE + SUBLANES, D), F32), pltpu.VMEM((RG_TILE, D), F32),
                        pltpu.VMEM((RG_TILE, D), F32), pltpu.VMEM((RG_TILE, D), F32)],
        input_output_aliases={0: 0},
        compiler_params=_params(("arbitrary", "arbitrary")),
        name="rg_prompt",
    )(x_all, *weights)

    h0e = jnp.repeat(state_h, DEC_SEQ, axis=0)
    buf8 = jnp.pad(state_conv, ((0, 0), (DEC_SEQ - (CONV_W - 1), 0), (0, 0))).reshape(N_SAMPLE, D)
    tile = pl.BlockSpec((TOK_TILE, D), lambda i: (i, 0))
    xtile = pl.BlockSpec((TOK_TILE, D), lambda i: (SAMPLE_BLK0 + i, 0))
    x_all, hs, raws = pl.pallas_call(
        _rg_sample_kernel,
        grid=(N_SAMPLE // TOK_TILE,),
        in_specs=[xtile, tile, tile] + wspecs,
        out_specs=[xtile, tile, tile],
        out_shape=[jax.ShapeDtypeStruct((N_TOT, D), F32),
                   jax.ShapeDtypeStruct((N_SAMPLE, D), F32),
                   jax.ShapeDtypeStruct((N_SAMPLE, D), F32)],
        input_output_aliases={0: 0},
        compiler_params=_params(("arbitrary",)),
        name="rg_sample",
    )(x_all, h0e, buf8, *weights)
    ht_s = hs.reshape(DEC_BATCH, DEC_SEQ, D)[:, DEC_SEQ - 1]
    cbuf_s = raws.reshape(DEC_BATCH, DEC_SEQ, D)[:, DEC_SEQ - (CONV_W - 1):]
    return x_all, ht_p.reshape(BATCH, D), cbuf_p, ht_s, cbuf_s


def _extract_top(s, k, out_ref):
    for i in range(k):
        m = jnp.max(s, axis=0, keepdims=True)
        out_ref[i:i + 1, :] = m
        s = jnp.where(s == m, -jnp.inf, s)


def _extract_top_sorted(s, k, out_ref):
    n = s.shape[0] // SUBLANES
    assert n == 16 and k <= n + 1
    v = [s[SUBLANES * g:SUBLANES * (g + 1), :] for g in range(n)]
    size = 2
    while size <= n:
        j = size // 2
        while j >= 1:
            for a in range(n):
                b = a ^ j
                if b > a:
                    hi, lo = jnp.maximum(v[a], v[b]), jnp.minimum(v[a], v[b])
                    v[a], v[b] = (hi, lo) if (a & size) == 0 else (lo, hi)
            j //= 2
        size *= 2
    neg = jnp.full(v[0].shape, -jnp.inf, F32)
    for i in range(k):
        m = jnp.max(v[0], axis=0, keepdims=True)
        out_ref[i:i + 1, :] = m
        if i == k - 1:
            break
        win = v[0] == m
        for d in range(min(n, k - 1 - i)):
            v[d] = jnp.where(win, v[d + 1] if d + 1 < n else neg, v[d])


def _router_kernel(x_ref, wqt_ref, k1_ref, k2_ref, ph_ref, cc_ref, e2_ref, v1s, v2s, cs):
    tt = x_ref.shape[0]
    xb = x_ref[...].astype(BF16)
    qt = lax.dot_general(wqt_ref[...], xb, NT_DIMS, preferred_element_type=F32)
    neg = jnp.full((3 * SUBLANES, tt), -jnp.inf, F32)
    rowi = lax.broadcasted_iota(jnp.int32, (SUBLANES, tt), 0)
    for h in range(PEER_HEADS):
        q1 = qt[h * 256:h * 256 + 128, :].astype(BF16)
        q2 = qt[h * 256 + 128:(h + 1) * 256, :].astype(BF16)
        s1 = jnp.dot(k1_ref[h], q1, preferred_element_type=F32)
        s2 = jnp.dot(k2_ref[h], q2, preferred_element_type=F32)
        v1s[...] = neg
        v2s[...] = neg
        _extract_top_sorted(s1, N_CAND, v1s)
        _extract_top_sorted(s2, N_CAND, v2s)
        v1 = v1s[...]
        v2 = v2s[...]
        blocks = [v1 + v2[0:1]]
        for b in range(1, SUBLANES):
            c = v1[0:SUBLANES] + v2[b:b + 1]
            lim = N_CAND // (b + 1)
            if lim < SUBLANES:
                c = jnp.where(rowi < lim, c, -jnp.inf)
            blocks.append(c)
        blocks.append(v2[SUBLANES:3 * SUBLANES] + v1[0:1])
        cs[...] = neg
        blocks.append(jnp.full((N_KEYS - 12 * SUBLANES, tt), -jnp.inf, F32))
        _extract_top_sorted(jnp.concatenate(blocks, axis=0), N_CAND, cs)
        top = cs[0:PEER_TOPK, :]
        z = jnp.sum(jnp.exp(top - top[0:1]), axis=0, keepdims=True)
        tau = 0.5 * (cs[PEER_TOPK - 1:PEER_TOPK, :] + cs[PEER_TOPK:PEER_TOPK + 1, :])
        ph = jnp.exp(tau - s1 - v2[0:1])
        cc = jnp.exp(s1 - v1[0:1]) * (0.5 / z)
        e2 = jnp.exp(s2 - v2[0:1])
        for c in range(tt // LANES):
            ls = slice(c * LANES, (c + 1) * LANES)
            ph_ref[h, c] = ph[:, ls]
            cc_ref[h, c] = cc[:, ls]
            e2_ref[h, c] = e2[:, ls]


def _expert_kernel(x_ref, u_ref, vt_ref, ph_ref, cc_ref, e2_ref, g_ref, b_ref, y_ref,
                   xb_s, acc_s, act_s):
    s = pl.program_id(1)
    ns = pl.num_programs(1)
    tt = x_ref.shape[0]
    rows_per_step = u_ref.shape[0] // N_KEYS
    assert rows_per_step == SUBLANES
    nchunk = tt // LANES
    group = 2

    def project(act_ref):
        act_ref[:, 0:tt] = lax.dot_general(u_ref[...], xb_s[...], NT_DIMS, preferred_element_type=F32)

    def gate_and_mix(act_ref):
        a0 = pl.multiple_of((s - 1) * rows_per_step, SUBLANES)
        phg = [[ph_ref[h, c, pl.ds(a0, SUBLANES), :] for h in range(PEER_HEADS)] for c in range(nchunk)]
        ccg = [[cc_ref[h, c, pl.ds(a0, SUBLANES), :] for h in range(PEER_HEADS)] for c in range(nchunk)]
        upd = None
        for r0 in range(0, rows_per_step, group):
            w_rows = []
            for r in range(r0, r0 + group):
                row = []
                for c in range(nchunk):
                    gsum = jnp.zeros((N_KEYS, LANES), F32)
                    for h in range(PEER_HEADS):
                        e2 = e2_ref[h, c]
                        sel = jnp.where(e2 >= phg[c][h][r:r + 1], e2, 0.0)
                        gsum = gsum + sel * ccg[c][h][r:r + 1]
                    a_rc = act_ref[r * N_KEYS:(r + 1) * N_KEYS, c * LANES:(c + 1) * LANES]
                    ga = gsum * a_rc
                    th = jnp.tanh(a_rc * (GELU_C1 + GELU_C3 * (a_rc * a_rc)))
                    row.append((ga + ga * th).astype(BF16))
                w_rows.append(jnp.concatenate(row, axis=1))
            w = jnp.concatenate(w_rows, axis=0)
            part = jnp.dot(vt_ref[:, r0 * N_KEYS:(r0 + group) * N_KEYS], w, preferred_element_type=F32)
            upd = part if upd is None else upd + part
        acc_s[...] += upd

    @pl.when(s == 0)
    def _():
        xb_s[...] = x_ref[...].astype(BF16)
        acc_s[...] = jnp.zeros_like(acc_s)
        project(act_s.at[0])

    @pl.when(jnp.logical_and(s > 0, s < ns - 1))
    def _():
        gate_and_mix(act_s.at[(s - 1) & 1])
        project(act_s.at[s & 1])

    @pl.when(s == ns - 1)
    def _():
        gate_and_mix(act_s.at[1])
        y_ref[...] = _layer_norm(ALPHA * x_ref[...] + acc_s[...].T, g_ref[...], b_ref[...])


def _peer_layer(x_all, w):
    nt = N_TOT // TOK_TILE
    cpt = TOK_TILE // LANES
    gate_shape = jax.ShapeDtypeStruct((PEER_HEADS, N_TOT // LANES, N_KEYS, LANES), F32)
    gate_spec = pl.BlockSpec((PEER_HEADS, cpt, N_KEYS, LANES), lambda i: (0, i, 0, 0))
    ph, cc, e2 = pl.pallas_call(
        _router_kernel,
        grid=(nt,),
        in_specs=[pl.BlockSpec((TOK_TILE, D), lambda i: (i, 0)),
                  _full(w["wq_t"].shape), _full(w["keys1"].shape), _full(w["keys2"].shape)],
        out_specs=[gate_spec] * 3,
        out_shape=[gate_shape] * 3,
        scratch_shapes=[pltpu.VMEM((3 * SUBLANES, TOK_TILE), F32)] * 3,
        compiler_params=_params(("arbitrary",)),
        name="peer_router",
    )(x_all, w["wq_t"], w["keys1"], w["keys2"])

    ne = N_EXPERTS // EXP_TILE
    assert ne % 2 == 0
    gate_spec2 = pl.BlockSpec((PEER_HEADS, cpt, N_KEYS, LANES), lambda i, s: (0, i, 0, 0))
    pitch = TOK_TILE + LANES
    return pl.pallas_call(
        _expert_kernel,
        grid=(nt, ne + 1),
        in_specs=[pl.BlockSpec((TOK_TILE, D), lambda i, s: (i, 0)),
                  pl.BlockSpec((EXP_TILE, D), lambda i, s: (jnp.minimum(s, ne - 1), 0)),
                  pl.BlockSpec((D, EXP_TILE), lambda i, s: (0, jnp.maximum(s - 1, 0))),
                  gate_spec2, gate_spec2, gate_spec2,
                  _full((1, D)), _full((1, D))],
        out_specs=pl.BlockSpec((TOK_TILE, D), lambda i, s: (i, 0)),
        out_shape=jax.ShapeDtypeStruct((N_TOT, D), F32),
        scratch_shapes=[pltpu.VMEM((TOK_TILE, D), BF16), pltpu.VMEM((D, TOK_TILE), F32),
                        pltpu.VMEM((2, EXP_TILE, pitch), F32)],
        compiler_params=_params(("arbitrary", "arbitrary")),
        name="peer_experts",
    )(x_all, w["u"], w["v_t"], ph, cc, e2, w["ln_g"], w["ln_b"])


def _cumsum(x, axis, stride=1):
    n = x.shape[axis]
    idx = lax.broadcasted_iota(jnp.int32, x.shape, axis)
    s = stride
    while s < n:
        x = x + jnp.where(idx >= s, pltpu.roll(x, s, axis), 0.0)
        s *= 2
    return x


def _log_forget(zf, bf):
    zf = zf + bf
    lf = jnp.minimum(zf, 0.0) - jnp.log1p(jnp.exp(-jnp.abs(zf)))
    lane = lax.broadcasted_iota(jnp.int32, lf.shape, 1)
    return jnp.where(lane < N_HEADS, lf, 0.0)


def _fox_proj_prompt_kernel(x_ref, w_ref, bf_ref, q_ref, khm_ref, vhm_ref, k_ref, v_ref, lf_ref, c_ref,
                            ccar):
    t = pl.program_id(1)
    rows = x_ref.shape[0]

    @pl.when(t == 0)
    def _():
        ccar[...] = jnp.zeros_like(ccar)

    z = jnp.dot(x_ref[...].astype(BF16), w_ref[...], preferred_element_type=F32)
    for h in range(N_HEADS):
        q_ref[0, h] = z[:, h * HEAD_DIM:(h + 1) * HEAD_DIM]
        khm_ref[0, h] = z[:, D + h * HEAD_DIM:D + (h + 1) * HEAD_DIM].astype(BF16)
        vhm_ref[0, h] = z[:, 2 * D + h * HEAD_DIM:2 * D + (h + 1) * HEAD_DIM].astype(BF16)
    k_ref[0] = z[:, D:2 * D]
    v_ref[0] = z[:, 2 * D:3 * D]
    lf = _log_forget(z[:, 3 * D:], bf_ref[...])
    lf_ref[0] = lf
    c = _cumsum(lf, 0) + ccar[0:1, :]
    c_ref[0] = c
    ccar[...] = jnp.broadcast_to(c[rows - 1:rows, :], ccar.shape)


def _fox_proj_sample_kernel(x_ref, w_ref, bf_ref, q_ref, k_ref, v_ref, lf_ref):
    z = jnp.dot(x_ref[...].astype(BF16), w_ref[...], preferred_element_type=F32)
    q_ref[...] = z[:, :D]
    k_ref[...] = z[:, D:2 * D]
    v_ref[...] = z[:, 2 * D:3 * D]
    lf_ref[...] = _log_forget(z[:, 3 * D:], bf_ref[...])


def _attn_prompt_kernel(q_ref, k_ref, v_ref, ck_ref, c_ref, o_ref, s_buf):
    hg = pl.program_id(1)
    qt = pl.program_id(2)
    lane = lax.broadcasted_iota(jnp.int32, (ATT_TQ, LANES), 1)
    c_blk = c_ref[0]
    qs = [q_ref[0, i].astype(BF16) for i in range(ATT_HPB)]
    cqs = [jnp.sum(jnp.where(lane == hg * ATT_HPB + i, c_blk, 0.0), axis=1, keepdims=True)
           for i in range(ATT_HPB)]
    qpos = qt * ATT_TQ + lax.broadcasted_iota(jnp.int32, (ATT_TQ, 1), 0)

    def logits(rows, tk, kpos0, ms):
        kpos = kpos0 + lax.broadcasted_iota(jnp.int32, (ATT_TQ, tk), 1)
        visible = kpos <= qpos
        out = []
        for i in range(ATT_HPB):
            s = lax.dot_general(qs[i], k_ref[0, i, rows, :], NT_DIMS, preferred_element_type=F32)
            s = jnp.where(visible, s * ATTN_SCALE + cqs[i] - ck_ref[0, i, :, rows], -jnp.inf)
            s_buf[i, :, rows] = s
            out.append(jnp.maximum(ms[i], jnp.max(s, axis=1, keepdims=True)))
        return tuple(out)

    def weigh(rows, ms, carry):
        out = []
        for i in range(ATT_HPB):
            l, acc = carry[2 * i:2 * i + 2]
            p = jnp.exp(s_buf[i, :, rows] - ms[i])
            l = l + jnp.sum(p, axis=1, keepdims=True)
            acc = acc + jnp.dot(p.astype(BF16), v_ref[0, i, rows, :], preferred_element_type=F32)
            out += [l, acc]
        return tuple(out)

    def chunk_rows(c):
        return pl.ds(pl.multiple_of(c * ATT_TK, ATT_TK), ATT_TK)

    n_main = (T_PROMPT // ATT_TK) * ATT_TK
    tail = slice(n_main, T_PROMPT)
    last_q = qt * ATT_TQ + (ATT_TQ - 1)
    n_chunks = jnp.minimum(lax.shift_right_logical(last_q, ATT_TK.bit_length() - 1) + 1, n_main // ATT_TK)

    ms = (jnp.full((ATT_TQ, 1), -jnp.inf, F32),) * ATT_HPB
    ms = lax.fori_loop(0, n_chunks, lambda c, m: logits(chunk_rows(c), ATT_TK, c * ATT_TK, m), ms)
    ms = logits(tail, T_PROMPT - n_main, n_main, ms)
    carry = (jnp.zeros((ATT_TQ, 1), F32), jnp.zeros((ATT_TQ, HEAD_DIM), F32)) * ATT_HPB
    carry = lax.fori_loop(0, n_chunks, lambda c, cr: weigh(chunk_rows(c), ms, cr), carry)
    carry = weigh(tail, ms, carry)
    for i in range(ATT_HPB):
        o_ref[0, :, i * HEAD_DIM:(i + 1) * HEAD_DIM] = carry[2 * i + 1] * (1.0 / carry[2 * i])


def _attn_sample_kernel(pt_ref, q_ref, *refs):
    del pt_ref
    kc_refs = refs[0:ATT_PPS]
    vc_refs = refs[ATT_PPS:2 * ATT_PPS]
    lf_refs = refs[2 * ATT_PPS:3 * ATT_PPS]
    kn_ref, vn_ref, lfn_ref, o_ref, qb, m_s, l_s, acc, car = refs[3 * ATT_PPS:]
    p = pl.program_id(1)
    np_ = pl.num_programs(1)
    nrow = DEC_SEQ * N_HEADS
    ncol = PAGE_SIZE * N_HEADS

    @pl.when(p == 0)
    def _():
        qb[...] = q_ref[0].astype(BF16)
        m_s[...] = jnp.full(m_s.shape, -jnp.inf, F32)
        l_s[...] = jnp.zeros_like(l_s)
        acc[...] = jnp.zeros_like(acc)
        car[...] = jnp.zeros_like(car)

    def update(kbs, vbs, biases, mask):
        q = qb[...]
        ss = [jnp.where(mask, lax.dot_general(q, kb, NT_DIMS, preferred_element_type=F32) * ATTN_SCALE - bias,
                        -jnp.inf) for kb, bias in zip(kbs, biases)]
        m_old = m_s[...]
        mn = m_old
        for sv in ss:
            mn = jnp.maximum(mn, jnp.max(sv, axis=1, keepdims=True))
        alpha = jnp.exp(m_old - mn)
        lsum = alpha * l_s[...]
        av = alpha * acc[...]
        for sv, vb in zip(ss, vbs):
            pr = jnp.exp(sv - mn)
            lsum = lsum + jnp.sum(pr, axis=1, keepdims=True)
            av = av + jnp.dot(pr.astype(BF16), vb, preferred_element_type=F32)
        l_s[...] = lsum
        acc[...] = av
        m_s[...] = mn

    def same_head(n):
        r = lax.broadcasted_iota(jnp.int32, (nrow, n), 0)
        c = lax.broadcasted_iota(jnp.int32, (nrow, n), 1)
        return (r & (N_HEADS - 1)) == (c & (N_HEADS - 1)), r, c

    run = car[...]
    biases = []
    for lf_ref in lf_refs:
        lf = lf_ref[0]
        biases.append(_cumsum(lf, 1, N_HEADS) + run)
        tot = lf
        sft = N_HEADS
        while sft < ncol:
            tot = tot + pltpu.roll(tot, sft, 1)
            sft *= 2
        run = run + tot
    car[...] = run
    mask, _, _ = same_head(ncol)
    update([r[0].astype(BF16) for r in kc_refs], [r[0].astype(BF16) for r in vc_refs], biases, mask)

    @pl.when(p == np_ - 1)
    def _():
        zpad = jnp.zeros((LANES - nrow, HEAD_DIM), F32)
        kb = jnp.concatenate([kn_ref[0], zpad], axis=0).astype(BF16)
        vb = jnp.concatenate([vn_ref[0], zpad], axis=0).astype(BF16)
        bias = _cumsum(lfn_ref[0], 1, N_HEADS) + car[:, 0:LANES]
        head_ok, r, c = same_head(LANES)
        causal = lax.shift_right_logical(c, 3) <= lax.shift_right_logical(r, 3)
        update([kb], [vb], [bias], head_ok & causal & (c < nrow))
        o_ref[0] = acc[...] * (1.0 / l_s[...])


def _out_proj_kernel(a_ref, x_ref, w_ref, g_ref, b_ref, y_ref):
    mix = jnp.dot(a_ref[...].astype(BF16), w_ref[...], preferred_element_type=F32)
    y_ref[...] = _layer_norm(ALPHA * x_ref[...] + mix, g_ref[...], b_ref[...])


def _out_proj(x_all, a, w_o, ln_g, ln_b, tile, blk0):
    n = a.shape[0] // tile
    xspec = pl.BlockSpec((tile, D), lambda i: (blk0 + i, 0))
    return pl.pallas_call(
        _out_proj_kernel,
        grid=(n,),
        in_specs=[pl.BlockSpec((tile, D), lambda i: (i, 0)), xspec, _full((D, D)), _full((1, D)),
                  _full((1, D))],
        out_specs=xspec,
        out_shape=jax.ShapeDtypeStruct((N_TOT, D), F32),
        input_output_aliases={1: 0},
        compiler_params=_params(("arbitrary",)),
        name="fox_out_proj",
    )(a, x_all, w_o, ln_g, ln_b)


def _fox_layer(x_all, cache_k, cache_v, cache_lf, page_table, w):
    nt = T_PROMPT // FOX_TILE
    hm_shape = jax.ShapeDtypeStruct((BATCH, N_HEADS, T_PROMPT, HEAD_DIM), BF16)
    hm_spec = pl.BlockSpec((1, N_HEADS, FOX_TILE, HEAD_DIM), lambda b, t: (b, 0, t, 0))
    nat_shape = jax.ShapeDtypeStruct((BATCH, T_PROMPT, D), F32)
    nat_spec = pl.BlockSpec((1, FOX_TILE, D), lambda b, t: (b, t, 0))
    lf_shape = jax.ShapeDtypeStruct((BATCH, T_PROMPT, LANES), F32)
    lf_spec = pl.BlockSpec((1, FOX_TILE, LANES), lambda b, t: (b, t, 0))
    q_hm, k_hm, v_hm, k_p, v_p, lf_p, c_p = pl.pallas_call(
        _fox_proj_prompt_kernel,
        grid=(BATCH, nt),
        in_specs=[pl.BlockSpec((FOX_TILE, D), lambda b, t: (b * nt + t, 0)),
                  _full((D, FOX_NPAD)), _full((1, LANES))],
        out_specs=[hm_spec, hm_spec, hm_spec, nat_spec, nat_spec, lf_spec, lf_spec],
        out_shape=[jax.ShapeDtypeStruct(hm_shape.shape, F32), hm_shape, hm_shape, nat_shape, nat_shape,
                   lf_shape, lf_shape],
        scratch_shapes=[pltpu.VMEM((SUBLANES, LANES), F32)],
        compiler_params=_params(("arbitrary", "arbitrary")),
        name="fox_proj_prompt",
    )(x_all, w["w_in"], w["b_f"])

    ck_row = jnp.pad(jnp.transpose(c_p[:, :, :N_HEADS], (0, 2, 1)),
                     ((0, 0), (0, 0), (0, T_KPAD - T_PROMPT)))[:, :, None, :]
    nq = T_PROMPT // ATT_TQ
    kv_spec = pl.BlockSpec((1, ATT_HPB, T_PROMPT, HEAD_DIM), lambda b, h, q: (b, h, 0, 0))
    attn_p = pl.pallas_call(
        _attn_prompt_kernel,
        grid=(BATCH, N_HEADS // ATT_HPB, nq),
        in_specs=[pl.BlockSpec((1, ATT_HPB, ATT_TQ, HEAD_DIM), lambda b, h, q: (b, h, q, 0)),
                  kv_spec, kv_spec,
                  pl.BlockSpec((1, ATT_HPB, 1, T_KPAD), lambda b, h, q: (b, h, 0, 0)),
                  pl.BlockSpec((1, ATT_TQ, LANES), lambda b, h, q: (b, q, 0))],
        out_specs=pl.BlockSpec((1, ATT_TQ, ATT_HPB * HEAD_DIM), lambda b, h, q: (b, q, h)),
        out_shape=jax.ShapeDtypeStruct((BATCH, T_PROMPT, D), F32),
        scratch_shapes=[pltpu.VMEM((ATT_HPB, ATT_TQ, T_KPAD), F32)],
        compiler_params=_params(("arbitrary", "arbitrary", "arbitrary")),
        name="fox_attn_prompt",
    )(q_hm, k_hm, v_hm, ck_row, c_p)

    ns = N_SAMPLE // TOK_TILE
    tile = pl.BlockSpec((TOK_TILE, D), lambda i: (i, 0))
    q_s, k_s, v_s, lf_s = pl.pallas_call(
        _fox_proj_sample_kernel,
        grid=(ns,),
        in_specs=[pl.BlockSpec((TOK_TILE, D), lambda i: (SAMPLE_BLK0 + i, 0)),
                  _full((D, FOX_NPAD)), _full((1, LANES))],
        out_specs=[tile, tile, tile, pl.BlockSpec((TOK_TILE, LANES), lambda i: (i, 0))],
        out_shape=[jax.ShapeDtypeStruct((N_SAMPLE, D), F32), jax.ShapeDtypeStruct((N_SAMPLE, D), F32),
                   jax.ShapeDtypeStruct((N_SAMPLE, D), F32), jax.ShapeDtypeStruct((N_SAMPLE, LANES), F32)],
        compiler_params=_params(("arbitrary",)),
        name="fox_proj_sample",
    )(x_all, w["w_in"], w["b_f"])

    nrow = DEC_SEQ * N_HEADS
    ncol = PAGE_SIZE * N_HEADS
    by_head = lambda a: a.reshape(DEC_BATCH, nrow, HEAD_DIM)
    lfn = jnp.pad(lf_s[:, :N_HEADS].reshape(DEC_BATCH, 1, nrow), ((0, 0), (0, 0), (0, LANES - nrow)))
    seq_spec = pl.BlockSpec((1, nrow, HEAD_DIM), lambda s, p, pt: (s, 0, 0))
    page_specs = [pl.BlockSpec((1, ncol, HEAD_DIM), lambda s, p, pt, i=i: (pt[s, p * ATT_PPS + i], 0, 0))
                  for i in range(ATT_PPS)]
    lf_specs = [pl.BlockSpec((1, 1, ncol), lambda s, p, pt, i=i: (pt[s, p * ATT_PPS + i], 0, 0))
                for i in range(ATT_PPS)]
    attn_s = pl.pallas_call(
        _attn_sample_kernel,
        grid_spec=pltpu.PrefetchScalarGridSpec(
            num_scalar_prefetch=1,
            grid=(DEC_BATCH, N_PAGES // ATT_PPS),
            in_specs=[seq_spec] + page_specs + page_specs + lf_specs + [
                seq_spec, seq_spec, pl.BlockSpec((1, 1, LANES), lambda s, p, pt: (s, 0, 0))],
            out_specs=seq_spec,
            scratch_shapes=[pltpu.VMEM((nrow, HEAD_DIM), BF16), pltpu.VMEM((nrow, 1), F32),
                            pltpu.VMEM((nrow, 1), F32), pltpu.VMEM((nrow, HEAD_DIM), F32),
                            pltpu.VMEM((1, ncol), F32)]),
        out_shape=jax.ShapeDtypeStruct((DEC_BATCH, nrow, HEAD_DIM), F32),
        compiler_params=_params(("arbitrary", "arbitrary")),
        name="fox_attn_sample",
    )(page_table, by_head(q_s), *([cache_k] * ATT_PPS + [cache_v] * ATT_PPS + [cache_lf] * ATT_PPS),
      by_head(k_s), by_head(v_s), lfn)

    x_all = _out_proj(x_all, attn_p.reshape(N_PROMPT, D), w["w_o"], w["ln_g"], w["ln_b"], FOX_TILE, 0)
    x_all = _out_proj(x_all, attn_s.reshape(N_SAMPLE, D), w["w_o"], w["ln_g"], w["ln_b"], TOK_TILE,
                      SAMPLE_BLK0)
    return x_all, k_p, v_p, lf_p[:, :, :N_HEADS], k_s, v_s, lf_s[:, :N_HEADS]


def _row(v):
    return v.reshape(1, -1).astype(F32)


def kernel(x_prompt, x_sample, cache_k, cache_v, cache_logf, state_h, state_conv, page_table, meta_tokens,
           rg_w_in, rg_conv_w, rg_conv_b, rg_w_a, rg_b_a, rg_w_x, rg_b_x, rg_lambda, rg_w_out, fox_w_in,
           fox_b_f, fox_w_o, peer_w_q, peer_keys1, peer_keys2, peer_u, peer_v, ln_mix_g, ln_mix_b,
           ln_ffn_g, ln_ffn_b):
    meta = jnp.broadcast_to(meta_tokens.astype(F32)[None], (BATCH, N_META, D))
    hp = jnp.concatenate([meta, x_prompt], axis=1).reshape(N_PROMPT, D)
    x_all = jnp.concatenate([hp, jnp.zeros((SAMPLE_OFF - N_PROMPT, D), F32),
                             x_sample.reshape(N_SAMPLE, D)], axis=0)

    def peer_weights(i):
        return {"wq_t": peer_w_q[i].T.astype(BF16), "keys1": peer_keys1[i].astype(BF16),
                "keys2": peer_keys2[i].astype(BF16), "u": peer_u[i].astype(BF16),
                "v_t": peer_v[i].T.astype(BF16), "ln_g": _row(ln_ffn_g[i]), "ln_b": _row(ln_ffn_b[i])}

    rg = {"w_in": rg_w_in[0].astype(BF16), "conv_w": rg_conv_w[0].astype(F32), "conv_b": _row(rg_conv_b[0]),
          "w_ax": jnp.concatenate([rg_w_a[0], rg_w_x[0]], axis=-1).astype(BF16),
          "b_a": _row(rg_b_a[0]), "b_x": _row(rg_b_x[0]), "lam": _row(rg_lambda[0]),
          "w_out": rg_w_out[0].astype(BF16), "ln_g": _row(ln_mix_g[0]), "ln_b": _row(ln_mix_b[0])}
    x_all, ht_p, cbuf_p, ht_s, cbuf_s = _rg_layer(x_all, state_h[0], state_conv[0], rg)
    x_all = _peer_layer(x_all, peer_weights(0))

    n_pool = cache_k.shape[1]
    fox = {"w_in": jnp.pad(fox_w_in[0], ((0, 0), (0, FOX_NPAD - fox_w_in.shape[-1]))).astype(BF16),
           "b_f": jnp.pad(fox_b_f[0], (0, LANES - N_HEADS)).reshape(1, LANES).astype(F32),
           "w_o": fox_w_o[0].astype(BF16), "ln_g": _row(ln_mix_g[1]), "ln_b": _row(ln_mix_b[1])}
    x_all, k_p, v_p, lf_p, k_s, v_s, lf_s = _fox_layer(
        x_all, cache_k.reshape(n_pool, PAGE_SIZE * N_HEADS, HEAD_DIM),
        cache_v.reshape(n_pool, PAGE_SIZE * N_HEADS, HEAD_DIM),
        cache_logf.reshape(n_pool, 1, PAGE_SIZE * N_HEADS), page_table, fox)
    x_all = _peer_layer(x_all, peer_weights(1))

    y_prompt = x_all[:N_PROMPT].reshape(BATCH, T_PROMPT, D)[:, N_META:]
    y_sample = x_all[SAMPLE_OFF:].reshape(DEC_BATCH, DEC_SEQ, D)
    hd = (N_HEADS, HEAD_DIM)
    return (y_prompt, y_sample,
            ht_p[None], cbuf_p[None],
            k_p.reshape(1, BATCH, T_PROMPT, *hd), v_p.reshape(1, BATCH, T_PROMPT, *hd), lf_p[None],
            ht_s[None], cbuf_s[None],
            k_s.reshape(1, DEC_BATCH, DEC_SEQ, *hd), v_s.reshape(1, DEC_BATCH, DEC_SEQ, *hd),
            lf_s.reshape(1, DEC_BATCH, DEC_SEQ, N_HEADS))
```

```python
import functools

import jax
import jax.numpy as jnp
from jax import lax
from jax.experimental import pallas as pl
from jax.experimental.pallas import tpu as pltpu

F32 = jnp.float32
BF16 = jnp.bfloat16

D = 1024
BATCH = 8
N_META = 16
T_PROMPT = 2064
DEC_BATCH = 128
DEC_SEQ = 8
N_HEADS = 8
HEAD_DIM = 128
PAGE_SIZE = 128
N_PAGES = 16
RG_BLOCKS = 8
RG_BLOCK = 128
RG_C = 8.0
CONV_W = 4
PEER_HEADS = 8
N_KEYS = 128
N_EXPERTS = N_KEYS * N_KEYS
PEER_TOPK = 16
DEPTH = 2
ALPHA = (2.0 * DEPTH) ** 0.25
LN_EPS = 1e-5
ATTN_SCALE = HEAD_DIM ** -0.5

SUBLANES = 8
LANES = 128
VMEM_LIMIT = 56 * 1024 * 1024

N_PROMPT = BATCH * T_PROMPT
TOK_TILE = 512
SAMPLE_OFF = 16896
N_SAMPLE = DEC_BATCH * DEC_SEQ
N_TOT = SAMPLE_OFF + N_SAMPLE
SAMPLE_BLK0 = SAMPLE_OFF // TOK_TILE

RG_TILE = 344
FOX_TILE = 688
ATT_TQ = 344
ATT_TK = 512
ATT_HPB = 4
ATT_PPS = 16
T_KPAD = 2304
EXP_TILE = 1024
N_CAND = 17
FOX_NPAD = 3 * D + LANES

NT_DIMS = (((1,), (1,)), ((), ()))


GELU_C1 = 0.7978845608028654
GELU_C3 = GELU_C1 * 0.044715


def _gelu(x):
    return 0.5 * x * (1.0 + jnp.tanh(x * (GELU_C1 + GELU_C3 * (x * x))))


def _sigmoid(x):
    return 1.0 / (1.0 + jnp.exp(-x))


def _softplus(x):
    return jnp.maximum(x, 0.0) + jnp.log1p(jnp.exp(-jnp.abs(x)))


def _layer_norm(x, g, b):
    mu = jnp.mean(x, axis=-1, keepdims=True)
    xc = x - mu
    var = jnp.mean(xc * xc, axis=-1, keepdims=True)
    return xc * lax.rsqrt(var + LN_EPS) * g + b


def _full(shape):
    n = len(shape)
    return pl.BlockSpec(shape, lambda *_: (0,) * n)


def _params(sem, **kw):
    return pltpu.CompilerParams(dimension_semantics=sem, vmem_limit_bytes=VMEM_LIMIT, **kw)


def _rg_gates(rec, wax_ref, ba_ref, bx_ref, lam_ref):
    a_parts, i_parts = [], []
    for n in range(RG_BLOCKS):
        rn = rec[:, n * RG_BLOCK:(n + 1) * RG_BLOCK].astype(BF16)
        ax = jnp.dot(rn, wax_ref[n], preferred_element_type=F32)
        a_parts.append(ax[:, :RG_BLOCK])
        i_parts.append(ax[:, RG_BLOCK:])
    r = _sigmoid(jnp.concatenate(a_parts, axis=1) + ba_ref[...])
    i = _sigmoid(jnp.concatenate(i_parts, axis=1) + bx_ref[...])
    log_a = (-RG_C) * r * _softplus(-lam_ref[...])
    a = jnp.exp(log_a)
    bx = jnp.sqrt(1.0 - a * a) * (i * rec)
    return a, bx


def _group_scan(a, b):
    row = lax.broadcasted_iota(jnp.int32, a.shape, 0) & (SUBLANES - 1)
    for s in (1, 2, 4):
        a_sh = pltpu.roll(a, s, 0)
        b_sh = pltpu.roll(b, s, 0)
        m = row >= s
        b = jnp.where(m, a * b_sh + b, b)
        a = jnp.where(m, a * a_sh, a)
    return a, b


def _rg_prompt_kernel(x_ref, win_ref, cw_ref, cb_ref, wax_ref, ba_ref, bx_ref, lam_ref, wout_ref,
                      g_ref, b_ref, y_ref, ht_ref, cbuf_ref, hcar, tail, cs, a_s, b_s, h_s):
    t = pl.program_id(1)
    nt = pl.num_programs(1)
    rows = x_ref.shape[0]

    @pl.when(t == 0)
    def _():
        hcar[...] = jnp.zeros_like(hcar)
        tail[...] = jnp.zeros_like(tail)

    x = x_ref[...]
    xin = jnp.dot(x.astype(BF16), win_ref[...], preferred_element_type=F32)
    gate = _gelu(xin[:, :D])
    raw = xin[:, D:]
    cs[0:SUBLANES, :] = tail[...]
    cs[SUBLANES:SUBLANES + rows, :] = raw
    cw = cw_ref[...]
    rec = (cb_ref[...] + cw[3:4] * raw + cw[2:3] * cs[7:7 + rows, :]
           + cw[1:2] * cs[6:6 + rows, :] + cw[0:1] * cs[5:5 + rows, :])
    tail[...] = raw[rows - SUBLANES:rows, :]

    a, bx = _rg_gates(rec, wax_ref, ba_ref, bx_ref, lam_ref)
    a, bx = _group_scan(a, bx)
    a_s[...] = a
    b_s[...] = bx

    def body(g, h):
        r0 = pl.multiple_of(g * SUBLANES, SUBLANES)
        hg = a_s[pl.ds(r0, SUBLANES), :] * h + b_s[pl.ds(r0, SUBLANES), :]
        h_s[pl.ds(r0, SUBLANES), :] = hg
        return jnp.broadcast_to(hg[SUBLANES - 1:SUBLANES, :], (SUBLANES, D))

    hl = lax.fori_loop(0, rows // SUBLANES, body, hcar[...])
    hcar[...] = hl

    y = h_s[...] * gate
    mix = jnp.dot(y.astype(BF16), wout_ref[...], preferred_element_type=F32)
    y_ref[...] = _layer_norm(ALPHA * x + mix, g_ref[...], b_ref[...])

    @pl.when(t == nt - 1)
    def _():
        ht_ref[0] = hl[0:1, :]
        cbuf_ref[0] = cs[rows + 5:rows + 8, :]


def _rg_sample_kernel(x_ref, h0_ref, buf_ref, win_ref, cw_ref, cb_ref, wax_ref, ba_ref, bx_ref, lam_ref,
                      wout_ref, g_ref, b_ref, y_ref, hs_ref, raw_ref):
    rows = x_ref.shape[0]
    x = x_ref[...]
    xin = jnp.dot(x.astype(BF16), win_ref[...], preferred_element_type=F32)
    gate = _gelu(xin[:, :D])
    raw = xin[:, D:]
    tmod = lax.broadcasted_iota(jnp.int32, raw.shape, 0) & (SUBLANES - 1)
    bufv = buf_ref[...]
    cw = cw_ref[...]
    rec = cb_ref[...] + cw[3:4] * raw
    for j in (1, 2, 3):
        sh = pltpu.roll(raw, j, 0)
        bsh = pltpu.roll(bufv, rows + j - SUBLANES, 0)
        rec = rec + cw[3 - j:4 - j] * jnp.where(tmod >= j, sh, bsh)
    a, bx = _rg_gates(rec, wax_ref, ba_ref, bx_ref, lam_ref)
    a, bx = _group_scan(a, bx)
    hh = a * h0_ref[...] + bx
    mix = jnp.dot((hh * gate).astype(BF16), wout_ref[...], preferred_element_type=F32)
    y_ref[...] = _layer_norm(ALPHA * x + mix, g_ref[...], b_ref[...])
    hs_ref[...] = hh
    raw_ref[...] = raw


def _rg_layer(x_all, state_h, state_conv, w):
    weights = [w["w_in"], w["conv_w"], w["conv_b"], w["w_ax"], w["b_a"], w["b_x"], w["lam"], w["w_out"],
               w["ln_g"], w["ln_b"]]
    wspecs = [_full(a.shape) for a in weights]
    nt = T_PROMPT // RG_TILE
    x_all, ht_p, cbuf_p = pl.pallas_call(
        _rg_prompt_kernel,
        grid=(BATCH, nt),
        in_specs=[pl.BlockSpec((RG_TILE, D), lambda b, t: (b * nt + t, 0))] + wspecs,
        out_specs=[pl.BlockSpec((RG_TILE, D), lambda b, t: (b * nt + t, 0)),
                   pl.BlockSpec((1, 1, D), lambda b, t: (b, 0, 0)),
                   pl.BlockSpec((1, CONV_W - 1, D), lambda b, t: (b, 0, 0))],
        out_shape=[jax.ShapeDtypeStruct((N_TOT, D), F32),
                   jax.ShapeDtypeStruct((BATCH, 1, D), F32),
                   jax.ShapeDtypeStruct((BATCH, CONV_W - 1, D), F32)],
        scratch_shapes=[pltpu.VMEM((SUBLANES, D), F32), pltpu.VMEM((SUBLANES, D), F32),
                        pltpu.VMEM((RG_TILE + SUBLANES, D), F32), pltpu.VMEM((RG_TILE, D), F32),
                        pltpu.VMEM((RG_TILE, D), F32), pltpu.VMEM((RG_TILE, D), F32)],
        input_output_aliases={0: 0},
        compiler_params=_params(("arbitrary", "arbitrary")),
        name="rg_prompt",
    )(x_all, *weights)

    h0e = jnp.repeat(state_h, DEC_SEQ, axis=0)
    buf8 = jnp.pad(state_conv, ((0, 0), (DEC_SEQ - (CONV_W - 1), 0), (0, 0))).reshape(N_SAMPLE, D)
    tile = pl.BlockSpec((TOK_TILE, D), lambda i: (i, 0))
    xtile = pl.BlockSpec((TOK_TILE, D), lambda i: (SAMPLE_BLK0 + i, 0))
    x_all, hs, raws = pl.pallas_call(
        _rg_sample_kernel,
        grid=(N_SAMPLE // TOK_TILE,),
        in_specs=[xtile, tile, tile] + wspecs,
        out_specs=[xtile, tile, tile],
        out_shape=[jax.ShapeDtypeStruct((N_TOT, D), F32),
                   jax.ShapeDtypeStruct((N_SAMPLE, D), F32),
                   jax.ShapeDtypeStruct((N_SAMPLE, D), F32)],
        input_output_aliases={0: 0},
        compiler_params=_params(("arbitrary",)),
        name="rg_sample",
    )(x_all, h0e, buf8, *weights)
    ht_s = hs.reshape(DEC_BATCH, DEC_SEQ, D)[:, DEC_SEQ - 1]
    cbuf_s = raws.reshape(DEC_BATCH, DEC_SEQ, D)[:, DEC_SEQ - (CONV_W - 1):]
    return x_all, ht_p.reshape(BATCH, D), cbuf_p, ht_s, cbuf_s


def _extract_top(s, k, out_ref):
    for i in range(k):
        m = jnp.max(s, axis=0, keepdims=True)
        out_ref[i:i + 1, :] = m
        s = jnp.where(s == m, -jnp.inf, s)


def _extract_top_sorted(s, k, out_ref):
    n = s.shape[0] // SUBLANES
    assert n == 16 and k <= n + 1
    v = [s[SUBLANES * g:SUBLANES * (g + 1), :] for g in range(n)]
    size = 2
    while size <= n:
        j = size // 2
        while j >= 1:
            for a in range(n):
                b = a ^ j
                if b > a:
                    hi, lo = jnp.maximum(v[a], v[b]), jnp.minimum(v[a], v[b])
                    v[a], v[b] = (hi, lo) if (a & size) == 0 else (lo, hi)
            j //= 2
        size *= 2
    neg = jnp.full(v[0].shape, -jnp.inf, F32)
    for i in range(k):
        m = jnp.max(v[0], axis=0, keepdims=True)
        out_ref[i:i + 1, :] = m
        if i == k - 1:
            break
        win = v[0] == m
        for d in range(min(n, k - 1 - i)):
            v[d] = jnp.where(win, v[d + 1] if d + 1 < n else neg, v[d])


def _router_kernel(x_ref, wqt_ref, k1_ref, k2_ref, ph_ref, cc_ref, e2_ref, v1s, v2s, cs):
    tt = x_ref.shape[0]
    xb = x_ref[...].astype(BF16)
    qt = lax.dot_general(wqt_ref[...], xb, NT_DIMS, preferred_element_type=F32)
    neg = jnp.full((3 * SUBLANES, tt), -jnp.inf, F32)
    rowi = lax.broadcasted_iota(jnp.int32, (SUBLANES, tt), 0)
    for h in range(PEER_HEADS):
        q1 = qt[h * 256:h * 256 + 128, :].astype(BF16)
        q2 = qt[h * 256 + 128:(h + 1) * 256, :].astype(BF16)
        s1 = jnp.dot(k1_ref[h], q1, preferred_element_type=F32)
        s2 = jnp.dot(k2_ref[h], q2, preferred_element_type=F32)
        v1s[...] = neg
        v2s[...] = neg
        _extract_top_sorted(s1, N_CAND, v1s)
        _extract_top_sorted(s2, N_CAND, v2s)
        v1 = v1s[...]
        v2 = v2s[...]
        blocks = [v1 + v2[0:1]]
        for b in range(1, SUBLANES):
            c = v1[0:SUBLANES] + v2[b:b + 1]
            lim = N_CAND // (b + 1)
            if lim < SUBLANES:
                c = jnp.where(rowi < lim, c, -jnp.inf)
            blocks.append(c)
        blocks.append(v2[SUBLANES:3 * SUBLANES] + v1[0:1])
        cs[...] = neg
        blocks.append(jnp.full((N_KEYS - 12 * SUBLANES, tt), -jnp.inf, F32))
        _extract_top_sorted(jnp.concatenate(blocks, axis=0), N_CAND, cs)
        top = cs[0:PEER_TOPK, :]
        z = jnp.sum(jnp.exp(top - top[0:1]), axis=0, keepdims=True)
        tau = 0.5 * (cs[PEER_TOPK - 1:PEER_TOPK, :] + cs[PEER_TOPK:PEER_TOPK + 1, :])
        ph = jnp.exp(tau - s1 - v2[0:1])
        cc = jnp.exp(s1 - v1[0:1]) * (0.5 / z)
        e2 = jnp.exp(s2 - v2[0:1])
        for c in range(tt // LANES):
            ls = slice(c * LANES, (c + 1) * LANES)
            ph_ref[h, c] = ph[:, ls]
            cc_ref[h, c] = cc[:, ls]
            e2_ref[h, c] = e2[:, ls]


def _expert_kernel(x_ref, u_ref, vt_ref, ph_ref, cc_ref, e2_ref, g_ref, b_ref, y_ref,
                   xb_s, acc_s, act_s):
    s = pl.program_id(1)
    ns = pl.num_programs(1)
    tt = x_ref.shape[0]
    rows_per_step = u_ref.shape[0] // N_KEYS
    assert rows_per_step == SUBLANES
    nchunk = tt // LANES
    group = 2

    def project(act_ref):
        act_ref[:, 0:tt] = jnp.dot(u_ref[...], xb_s[...], preferred_element_type=F32)

    def gate_and_mix(act_ref):
        a0 = pl.multiple_of((s - 1) * rows_per_step, SUBLANES)
        phg = [[ph_ref[h, c, pl.ds(a0, SUBLANES), :] for h in range(PEER_HEADS)] for c in range(nchunk)]
        ccg = [[cc_ref[h, c, pl.ds(a0, SUBLANES), :] for h in range(PEER_HEADS)] for c in range(nchunk)]
        upd = None
        for r0 in range(0, rows_per_step, group):
            w_rows = []
            for r in range(r0, r0 + group):
                row = []
                for c in range(nchunk):
                    gsum = jnp.zeros((N_KEYS, LANES), F32)
                    for h in range(PEER_HEADS):
                        e2 = e2_ref[h, c]
                        sel = jnp.where(e2 >= phg[c][h][r:r + 1], e2, 0.0)
                        gsum = gsum + sel * ccg[c][h][r:r + 1]
                    a_rc = act_ref[r * N_KEYS:(r + 1) * N_KEYS, c * LANES:(c + 1) * LANES]
                    ga = gsum * a_rc
                    th = jnp.tanh(a_rc * (GELU_C1 + GELU_C3 * (a_rc * a_rc)))
                    row.append((ga + ga * th).astype(BF16))
                w_rows.append(jnp.concatenate(row, axis=1))
            w = jnp.concatenate(w_rows, axis=0)
            part = jnp.dot(vt_ref[:, r0 * N_KEYS:(r0 + group) * N_KEYS], w, preferred_element_type=F32)
            upd = part if upd is None else upd + part
        acc_s[...] += upd

    @pl.when(s == 0)
    def _():
        xb_s[...] = x_ref[...].T.astype(BF16)
        acc_s[...] = jnp.zeros_like(acc_s)
        project(act_s.at[0])

    @pl.when(jnp.logical_and(s > 0, s < ns - 1))
    def _():
        gate_and_mix(act_s.at[(s - 1) & 1])
        project(act_s.at[s & 1])

    @pl.when(s == ns - 1)
    def _():
        gate_and_mix(act_s.at[1])
        y_ref[...] = _layer_norm(ALPHA * x_ref[...] + acc_s[...].T, g_ref[...], b_ref[...])


def _peer_layer(x_all, w):
    nt = N_TOT // TOK_TILE
    cpt = TOK_TILE // LANES
    gate_shape = jax.ShapeDtypeStruct((PEER_HEADS, N_TOT // LANES, N_KEYS, LANES), F32)
    gate_spec = pl.BlockSpec((PEER_HEADS, cpt, N_KEYS, LANES), lambda i: (0, i, 0, 0))
    ph, cc, e2 = pl.pallas_call(
        _router_kernel,
        grid=(nt,),
        in_specs=[pl.BlockSpec((TOK_TILE, D), lambda i: (i, 0)),
                  _full(w["wq_t"].shape), _full(w["keys1"].shape), _full(w["keys2"].shape)],
        out_specs=[gate_spec] * 3,
        out_shape=[gate_shape] * 3,
        scratch_shapes=[pltpu.VMEM((3 * SUBLANES, TOK_TILE), F32)] * 3,
        compiler_params=_params(("arbitrary",)),
        name="peer_router",
    )(x_all, w["wq_t"], w["keys1"], w["keys2"])

    ne = N_EXPERTS // EXP_TILE
    assert ne % 2 == 0
    gate_spec2 = pl.BlockSpec((PEER_HEADS, cpt, N_KEYS, LANES), lambda i, s: (0, i, 0, 0))
    pitch = TOK_TILE + LANES
    return pl.pallas_call(
        _expert_kernel,
        grid=(nt, ne + 1),
        in_specs=[pl.BlockSpec((TOK_TILE, D), lambda i, s: (i, 0)),
                  pl.BlockSpec((EXP_TILE, D), lambda i, s: (jnp.minimum(s, ne - 1), 0)),
                  pl.BlockSpec((D, EXP_TILE), lambda i, s: (0, jnp.maximum(s - 1, 0))),
                  gate_spec2, gate_spec2, gate_spec2,
                  _full((1, D)), _full((1, D))],
        out_specs=pl.BlockSpec((TOK_TILE, D), lambda i, s: (i, 0)),
        out_shape=jax.ShapeDtypeStruct((N_TOT, D), F32),
        scratch_shapes=[pltpu.VMEM((D, TOK_TILE), BF16), pltpu.VMEM((D, TOK_TILE), F32),
                        pltpu.VMEM((2, EXP_TILE, pitch), F32)],
        compiler_params=_params(("arbitrary", "arbitrary")),
        name="peer_experts",
    )(x_all, w["u"], w["v_t"], ph, cc, e2, w["ln_g"], w["ln_b"])


def _cumsum(x, axis, stride=1):
    n = x.shape[axis]
    idx = lax.broadcasted_iota(jnp.int32, x.shape, axis)
    s = stride
    while s < n:
        x = x + jnp.where(idx >= s, pltpu.roll(x, s, axis), 0.0)
        s *= 2
    return x


def _log_forget(zf, bf):
    zf = zf + bf
    lf = jnp.minimum(zf, 0.0) - jnp.log1p(jnp.exp(-jnp.abs(zf)))
    lane = lax.broadcasted_iota(jnp.int32, lf.shape, 1)
    return jnp.where(lane < N_HEADS, lf, 0.0)


def _fox_proj_prompt_kernel(x_ref, w_ref, bf_ref, q_ref, khm_ref, vhm_ref, k_ref, v_ref, lf_ref, c_ref,
                            ccar):
    t = pl.program_id(1)
    rows = x_ref.shape[0]

    @pl.when(t == 0)
    def _():
        ccar[...] = jnp.zeros_like(ccar)

    z = jnp.dot(x_ref[...].astype(BF16), w_ref[...], preferred_element_type=F32)
    for h in range(N_HEADS):
        q_ref[0, h] = z[:, h * HEAD_DIM:(h + 1) * HEAD_DIM]
        khm_ref[0, h] = z[:, D + h * HEAD_DIM:D + (h + 1) * HEAD_DIM].astype(BF16)
        vhm_ref[0, h] = z[:, 2 * D + h * HEAD_DIM:2 * D + (h + 1) * HEAD_DIM].astype(BF16)
    k_ref[0] = z[:, D:2 * D]
    v_ref[0] = z[:, 2 * D:3 * D]
    lf = _log_forget(z[:, 3 * D:], bf_ref[...])
    lf_ref[0] = lf
    c = _cumsum(lf, 0) + ccar[0:1, :]
    c_ref[0] = c
    ccar[...] = jnp.broadcast_to(c[rows - 1:rows, :], ccar.shape)


def _fox_proj_sample_kernel(x_ref, w_ref, bf_ref, q_ref, k_ref, v_ref, lf_ref):
    z = jnp.dot(x_ref[...].astype(BF16), w_ref[...], preferred_element_type=F32)
    q_ref[...] = z[:, :D]
    k_ref[...] = z[:, D:2 * D]
    v_ref[...] = z[:, 2 * D:3 * D]
    lf_ref[...] = _log_forget(z[:, 3 * D:], bf_ref[...])


def _attn_prompt_kernel(q_ref, k_ref, v_ref, ck_ref, c_ref, o_ref, s_buf):
    hg = pl.program_id(1)
    qt = pl.program_id(2)
    lane = lax.broadcasted_iota(jnp.int32, (ATT_TQ, LANES), 1)
    c_blk = c_ref[0]
    qs = [q_ref[0, i].astype(BF16) for i in range(ATT_HPB)]
    cqs = [jnp.sum(jnp.where(lane == hg * ATT_HPB + i, c_blk, 0.0), axis=1, keepdims=True)
           for i in range(ATT_HPB)]
    qpos = qt * ATT_TQ + lax.broadcasted_iota(jnp.int32, (ATT_TQ, 1), 0)

    def logits(rows, tk, kpos0, ms):
        kpos = kpos0 + lax.broadcasted_iota(jnp.int32, (ATT_TQ, tk), 1)
        visible = kpos <= qpos
        out = []
        for i in range(ATT_HPB):
            s = lax.dot_general(qs[i], k_ref[0, i, rows, :], NT_DIMS, preferred_element_type=F32)
            s = jnp.where(visible, s * ATTN_SCALE + cqs[i] - ck_ref[0, i, :, rows], -jnp.inf)
            s_buf[i, :, rows] = s
            out.append(jnp.maximum(ms[i], jnp.max(s, axis=1, keepdims=True)))
        return tuple(out)

    def weigh(rows, ms, carry):
        out = []
        for i in range(ATT_HPB):
            l, acc = carry[2 * i:2 * i + 2]
            p = jnp.exp(s_buf[i, :, rows] - ms[i])
            l = l + jnp.sum(p, axis=1, keepdims=True)
            acc = acc + jnp.dot(p.astype(BF16), v_ref[0, i, rows, :], preferred_element_type=F32)
            out += [l, acc]
        return tuple(out)

    def chunk_rows(c):
        return pl.ds(pl.multiple_of(c * ATT_TK, ATT_TK), ATT_TK)

    n_main = (T_PROMPT // ATT_TK) * ATT_TK
    tail = slice(n_main, T_PROMPT)
    last_q = qt * ATT_TQ + (ATT_TQ - 1)
    n_chunks = jnp.minimum(lax.shift_right_logical(last_q, ATT_TK.bit_length() - 1) + 1, n_main // ATT_TK)

    ms = (jnp.full((ATT_TQ, 1), -jnp.inf, F32),) * ATT_HPB
    ms = lax.fori_loop(0, n_chunks, lambda c, m: logits(chunk_rows(c), ATT_TK, c * ATT_TK, m), ms)
    ms = logits(tail, T_PROMPT - n_main, n_main, ms)
    carry = (jnp.zeros((ATT_TQ, 1), F32), jnp.zeros((ATT_TQ, HEAD_DIM), F32)) * ATT_HPB
    carry = lax.fori_loop(0, n_chunks, lambda c, cr: weigh(chunk_rows(c), ms, cr), carry)
    carry = weigh(tail, ms, carry)
    for i in range(ATT_HPB):
        o_ref[0, :, i * HEAD_DIM:(i + 1) * HEAD_DIM] = carry[2 * i + 1] * (1.0 / carry[2 * i])


def _attn_sample_kernel(pt_ref, q_ref, *refs):
    del pt_ref
    kc_refs = refs[0:ATT_PPS]
    vc_refs = refs[ATT_PPS:2 * ATT_PPS]
    lf_refs = refs[2 * ATT_PPS:3 * ATT_PPS]
    kn_ref, vn_ref, lfn_ref, o_ref, qb, m_s, l_s, acc, car = refs[3 * ATT_PPS:]
    p = pl.program_id(1)
    np_ = pl.num_programs(1)
    nrow = DEC_SEQ * N_HEADS
    ncol = PAGE_SIZE * N_HEADS

    @pl.when(p == 0)
    def _():
        qb[...] = q_ref[0].astype(BF16)
        m_s[...] = jnp.full(m_s.shape, -jnp.inf, F32)
        l_s[...] = jnp.zeros_like(l_s)
        acc[...] = jnp.zeros_like(acc)
        car[...] = jnp.zeros_like(car)

    def update(kbs, vbs, biases, mask):
        q = qb[...]
        ss = [jnp.where(mask, lax.dot_general(q, kb, NT_DIMS, preferred_element_type=F32) * ATTN_SCALE - bias,
                        -jnp.inf) for kb, bias in zip(kbs, biases)]
        m_old = m_s[...]
        mn = m_old
        for sv in ss:
            mn = jnp.maximum(mn, jnp.max(sv, axis=1, keepdims=True))
        alpha = jnp.exp(m_old - mn)
        lsum = alpha * l_s[...]
        av = alpha * acc[...]
        for sv, vb in zip(ss, vbs):
            pr = jnp.exp(sv - mn)
            lsum = lsum + jnp.sum(pr, axis=1, keepdims=True)
            av = av + jnp.dot(pr.astype(BF16), vb, preferred_element_type=F32)
        l_s[...] = lsum
        acc[...] = av
        m_s[...] = mn

    def same_head(n):
        r = lax.broadcasted_iota(jnp.int32, (nrow, n), 0)
        c = lax.broadcasted_iota(jnp.int32, (nrow, n), 1)
        return (r & (N_HEADS - 1)) == (c & (N_HEADS - 1)), r, c

    run = car[...]
    biases = []
    for lf_ref in lf_refs:
        lf = lf_ref[0]
        biases.append(_cumsum(lf, 1, N_HEADS) + run)
        tot = lf
        sft = N_HEADS
        while sft < ncol:
            tot = tot + pltpu.roll(tot, sft, 1)
            sft *= 2
        run = run + tot
    car[...] = run
    mask, _, _ = same_head(ncol)
    update([r[0].astype(BF16) for r in kc_refs], [r[0].astype(BF16) for r in vc_refs], biases, mask)

    @pl.when(p == np_ - 1)
    def _():
        zpad = jnp.zeros((LANES - nrow, HEAD_DIM), F32)
        kb = jnp.concatenate([kn_ref[0], zpad], axis=0).astype(BF16)
        vb = jnp.concatenate([vn_ref[0], zpad], axis=0).astype(BF16)
        bias = _cumsum(lfn_ref[0], 1, N_HEADS) + car[:, 0:LANES]
        head_ok, r, c = same_head(LANES)
        causal = lax.shift_right_logical(c, 3) <= lax.shift_right_logical(r, 3)
        update([kb], [vb], [bias], head_ok & causal & (c < nrow))
        o_ref[0] = acc[...] * (1.0 / l_s[...])


def _out_proj_kernel(a_ref, x_ref, w_ref, g_ref, b_ref, y_ref):
    mix = jnp.dot(a_ref[...].astype(BF16), w_ref[...], preferred_element_type=F32)
    y_ref[...] = _layer_norm(ALPHA * x_ref[...] + mix, g_ref[...], b_ref[...])


def _out_proj(x_all, a, w_o, ln_g, ln_b, tile, blk0):
    n = a.shape[0] // tile
    xspec = pl.BlockSpec((tile, D), lambda i: (blk0 + i, 0))
    return pl.pallas_call(
        _out_proj_kernel,
        grid=(n,),
        in_specs=[pl.BlockSpec((tile, D), lambda i: (i, 0)), xspec, _full((D, D)), _full((1, D)),
                  _full((1, D))],
        out_specs=xspec,
        out_shape=jax.ShapeDtypeStruct((N_TOT, D), F32),
        input_output_aliases={1: 0},
        compiler_params=_params(("arbitrary",)),
        name="fox_out_proj",
    )(a, x_all, w_o, ln_g, ln_b)


def _fox_layer(x_all, cache_k, cache_v, cache_lf, page_table, w):
    nt = T_PROMPT // FOX_TILE
    hm_shape = jax.ShapeDtypeStruct((BATCH, N_HEADS, T_PROMPT, HEAD_DIM), BF16)
    hm_spec = pl.BlockSpec((1, N_HEADS, FOX_TILE, HEAD_DIM), lambda b, t: (b, 0, t, 0))
    nat_shape = jax.ShapeDtypeStruct((BATCH, T_PROMPT, D), F32)
    nat_spec = pl.BlockSpec((1, FOX_TILE, D), lambda b, t: (b, t, 0))
    lf_shape = jax.ShapeDtypeStruct((BATCH, T_PROMPT, LANES), F32)
    lf_spec = pl.BlockSpec((1, FOX_TILE, LANES), lambda b, t: (b, t, 0))
    q_hm, k_hm, v_hm, k_p, v_p, lf_p, c_p = pl.pallas_call(
        _fox_proj_prompt_kernel,
        grid=(BATCH, nt),
        in_specs=[pl.BlockSpec((FOX_TILE, D), lambda b, t: (b * nt + t, 0)),
                  _full((D, FOX_NPAD)), _full((1, LANES))],
        out_specs=[hm_spec, hm_spec, hm_spec, nat_spec, nat_spec, lf_spec, lf_spec],
        out_shape=[jax.ShapeDtypeStruct(hm_shape.shape, F32), hm_shape, hm_shape, nat_shape, nat_shape,
                   lf_shape, lf_shape],
        scratch_shapes=[pltpu.VMEM((SUBLANES, LANES), F32)],
        compiler_params=_params(("arbitrary", "arbitrary")),
        name="fox_proj_prompt",
    )(x_all, w["w_in"], w["b_f"])

    ck_row = jnp.pad(jnp.transpose(c_p[:, :, :N_HEADS], (0, 2, 1)),
                     ((0, 0), (0, 0), (0, T_KPAD - T_PROMPT)))[:, :, None, :]
    nq = T_PROMPT // ATT_TQ
    kv_spec = pl.BlockSpec((1, ATT_HPB, T_PROMPT, HEAD_DIM), lambda b, h, q: (b, h, 0, 0))
    attn_p = pl.pallas_call(
        _attn_prompt_kernel,
        grid=(BATCH, N_HEADS // ATT_HPB, nq),
        in_specs=[pl.BlockSpec((1, ATT_HPB, ATT_TQ, HEAD_DIM), lambda b, h, q: (b, h, q, 0)),
                  kv_spec, kv_spec,
                  pl.BlockSpec((1, ATT_HPB, 1, T_KPAD), lambda b, h, q: (b, h, 0, 0)),
                  pl.BlockSpec((1, ATT_TQ, LANES), lambda b, h, q: (b, q, 0))],
        out_specs=pl.BlockSpec((1, ATT_TQ, ATT_HPB * HEAD_DIM), lambda b, h, q: (b, q, h)),
        out_shape=jax.ShapeDtypeStruct((BATCH, T_PROMPT, D), F32),
        scratch_shapes=[pltpu.VMEM((ATT_HPB, ATT_TQ, T_KPAD), F32)],
        compiler_params=_params(("arbitrary", "arbitrary", "arbitrary")),
        name="fox_attn_prompt",
    )(q_hm, k_hm, v_hm, ck_row, c_p)

    ns = N_SAMPLE // TOK_TILE
    tile = pl.BlockSpec((TOK_TILE, D), lambda i: (i, 0))
    q_s, k_s, v_s, lf_s = pl.pallas_call(
        _fox_proj_sample_kernel,
        grid=(ns,),
        in_specs=[pl.BlockSpec((TOK_TILE, D), lambda i: (SAMPLE_BLK0 + i, 0)),
                  _full((D, FOX_NPAD)), _full((1, LANES))],
        out_specs=[tile, tile, tile, pl.BlockSpec((TOK_TILE, LANES), lambda i: (i, 0))],
        out_shape=[jax.ShapeDtypeStruct((N_SAMPLE, D), F32), jax.ShapeDtypeStruct((N_SAMPLE, D), F32),
                   jax.ShapeDtypeStruct((N_SAMPLE, D), F32), jax.ShapeDtypeStruct((N_SAMPLE, LANES), F32)],
        compiler_params=_params(("arbitrary",)),
        name="fox_proj_sample",
    )(x_all, w["w_in"], w["b_f"])

    nrow = DEC_SEQ * N_HEADS
    ncol = PAGE_SIZE * N_HEADS
    by_head = lambda a: a.reshape(DEC_BATCH, nrow, HEAD_DIM)
    lfn = jnp.pad(lf_s[:, :N_HEADS].reshape(DEC_BATCH, 1, nrow), ((0, 0), (0, 0), (0, LANES - nrow)))
    seq_spec = pl.BlockSpec((1, nrow, HEAD_DIM), lambda s, p, pt: (s, 0, 0))
    page_specs = [pl.BlockSpec((1, ncol, HEAD_DIM), lambda s, p, pt, i=i: (pt[s, p * ATT_PPS + i], 0, 0))
                  for i in range(ATT_PPS)]
    lf_specs = [pl.BlockSpec((1, 1, ncol), lambda s, p, pt, i=i: (pt[s, p * ATT_PPS + i], 0, 0))
                for i in range(ATT_PPS)]
    attn_s = pl.pallas_call(
        _attn_sample_kernel,
        grid_spec=pltpu.PrefetchScalarGridSpec(
            num_scalar_prefetch=1,
            grid=(DEC_BATCH, N_PAGES // ATT_PPS),
            in_specs=[seq_spec] + page_specs + page_specs + lf_specs + [
                seq_spec, seq_spec, pl.BlockSpec((1, 1, LANES), lambda s, p, pt: (s, 0, 0))],
            out_specs=seq_spec,
            scratch_shapes=[pltpu.VMEM((nrow, HEAD_DIM), BF16), pltpu.VMEM((nrow, 1), F32),
                            pltpu.VMEM((nrow, 1), F32), pltpu.VMEM((nrow, HEAD_DIM), F32),
                            pltpu.VMEM((1, ncol), F32)]),
        out_shape=jax.ShapeDtypeStruct((DEC_BATCH, nrow, HEAD_DIM), F32),
        compiler_params=_params(("arbitrary", "arbitrary")),
        name="fox_attn_sample",
    )(page_table, by_head(q_s), *([cache_k] * ATT_PPS + [cache_v] * ATT_PPS + [cache_lf] * ATT_PPS),
      by_head(k_s), by_head(v_s), lfn)

    x_all = _out_proj(x_all, attn_p.reshape(N_PROMPT, D), w["w_o"], w["ln_g"], w["ln_b"], FOX_TILE, 0)
    x_all = _out_proj(x_all, attn_s.reshape(N_SAMPLE, D), w["w_o"], w["ln_g"], w["ln_b"], TOK_TILE,
                      SAMPLE_BLK0)
    return x_all, k_p, v_p, lf_p[:, :, :N_HEADS], k_s, v_s, lf_s[:, :N_HEADS]


def _row(v):
    return v.reshape(1, -1).astype(F32)


def kernel(x_prompt, x_sample, cache_k, cache_v, cache_logf, state_h, state_conv, page_table, meta_tokens,
           rg_w_in, rg_conv_w, rg_conv_b, rg_w_a, rg_b_a, rg_w_x, rg_b_x, rg_lambda, rg_w_out, fox_w_in,
           fox_b_f, fox_w_o, peer_w_q, peer_keys1, peer_keys2, peer_u, peer_v, ln_mix_g, ln_mix_b,
           ln_ffn_g, ln_ffn_b):
    meta = jnp.broadcast_to(meta_tokens.astype(F32)[None], (BATCH, N_META, D))
    hp = jnp.concatenate([meta, x_prompt], axis=1).reshape(N_PROMPT, D)
    x_all = jnp.concatenate([hp, jnp.zeros((SAMPLE_OFF - N_PROMPT, D), F32),
                             x_sample.reshape(N_SAMPLE, D)], axis=0)

    def peer_weights(i):
        return {"wq_t": peer_w_q[i].T.astype(BF16), "keys1": peer_keys1[i].astype(BF16),
                "keys2": peer_keys2[i].astype(BF16), "u": peer_u[i].astype(BF16),
                "v_t": peer_v[i].T.astype(BF16), "ln_g": _row(ln_ffn_g[i]), "ln_b": _row(ln_ffn_b[i])}

    rg = {"w_in": rg_w_in[0].astype(BF16), "conv_w": rg_conv_w[0].astype(F32), "conv_b": _row(rg_conv_b[0]),
          "w_ax": jnp.concatenate([rg_w_a[0], rg_w_x[0]], axis=-1).astype(BF16),
          "b_a": _row(rg_b_a[0]), "b_x": _row(rg_b_x[0]), "lam": _row(rg_lambda[0]),
          "w_out": rg_w_out[0].astype(BF16), "ln_g": _row(ln_mix_g[0]), "ln_b": _row(ln_mix_b[0])}
    x_all, ht_p, cbuf_p, ht_s, cbuf_s = _rg_layer(x_all, state_h[0], state_conv[0], rg)
    x_all = _peer_layer(x_all, peer_weights(0))

    n_pool = cache_k.shape[1]
    fox = {"w_in": jnp.pad(fox_w_in[0], ((0, 0), (0, FOX_NPAD - fox_w_in.shape[-1]))).astype(BF16),
           "b_f": jnp.pad(fox_b_f[0], (0, LANES - N_HEADS)).reshape(1, LANES).astype(F32),
           "w_o": fox_w_o[0].astype(BF16), "ln_g": _row(ln_mix_g[1]), "ln_b": _row(ln_mix_b[1])}
    x_all, k_p, v_p, lf_p, k_s, v_s, lf_s = _fox_layer(
        x_all, cache_k.reshape(n_pool, PAGE_SIZE * N_HEADS, HEAD_DIM),
        cache_v.reshape(n_pool, PAGE_SIZE * N_HEADS, HEAD_DIM),
        cache_logf.reshape(n_pool, 1, PAGE_SIZE * N_HEADS), page_table, fox)
    x_all = _peer_layer(x_all, peer_weights(1))

    y_prompt = x_all[:N_PROMPT].reshape(BATCH, T_PROMPT, D)[:, N_META:]
    y_sample = x_all[SAMPLE_OFF:].reshape(DEC_BATCH, DEC_SEQ, D)
    hd = (N_HEADS, HEAD_DIM)
    return (y_prompt, y_sample,
            ht_p[None], cbuf_p[None],
            k_p.reshape(1, BATCH, T_PROMPT, *hd), v_p.reshape(1, BATCH, T_PROMPT, *hd), lf_p[None],
            ht_s[None], cbuf_s[None],
            k_s.reshape(1, DEC_BATCH, DEC_SEQ, *hd), v_s.reshape(1, DEC_BATCH, DEC_SEQ, *hd),
            lf_s.reshape(1, DEC_BATCH, DEC_SEQ, N_HEADS))
```
